```python
import jax, jax.numpy as jnp
from jax import lax
import numpy as np

D_MODEL = 1024
BATCH = 16
SEQ = 4096
DEPTH = 1
DEC_BATCH = 2
DEC_SEQ = 16384
PAST_LEN = 128

HEAD_DIM = 64
A_Q_HEADS = 8
A_KV_HEADS = 2
A_WINDOW = 128
B_GROUPS = ((128, 1), (512, 4), (2048, 16))
B_HEADS_PER_GROUP = 4
B_HEADS = B_HEADS_PER_GROUP * len(B_GROUPS)
D_FF = ((8 * D_MODEL + 3 * 256 - 1) // (3 * 256)) * 256
ROPE_THETA = 10000.0
EPS = 1e-6
NEG_INF = -1e30

A_Q_W = A_Q_HEADS * HEAD_DIM
A_KV_W = A_KV_HEADS * HEAD_DIM
B_W = B_HEADS * HEAD_DIM
B_OUT_W = B_HEADS_PER_GROUP * HEAD_DIM
GATE_W = 2 * D_MODEL
IN_SPLITS = [A_Q_W, A_Q_W + A_KV_W, A_Q_W + 2 * A_KV_W,
             A_Q_W + 2 * A_KV_W + B_W, A_Q_W + 2 * A_KV_W + 2 * B_W,
             A_Q_W + 2 * A_KV_W + 3 * B_W]
IN_WIDTH = A_Q_W + 2 * A_KV_W + 3 * B_W + GATE_W

kernel_name = "gated_window_dilated_encoder"


def rms_norm(x, g):
    xf = x.astype(jnp.float32)
    y = xf * lax.rsqrt(jnp.mean(xf * xf, axis=-1, keepdims=True) + EPS) * g.astype(jnp.float32)
    return y.astype(x.dtype)


def rope_tables(seq_len):
    inv_freq = ROPE_THETA ** (-jnp.arange(0, HEAD_DIM, 2, dtype=jnp.float32) / HEAD_DIM)
    ang = jnp.arange(seq_len, dtype=jnp.float32)[:, None] * inv_freq[None, :]
    ang = jnp.concatenate([ang, ang], axis=-1)
    return jnp.cos(ang), jnp.sin(ang)


def apply_rope(t, cos, sin):
    tf = t.astype(jnp.float32)
    half = HEAD_DIM // 2
    rot = jnp.concatenate([-tf[..., half:], tf[..., :half]], axis=-1)
    return (tf * cos[None, :, None, :] + rot * sin[None, :, None, :]).astype(t.dtype)


def banded_attention(q, k, v, window, sink=None):
    n, L, hq, hd = q.shape
    hkv = k.shape[2]
    g = hq // hkv
    blk = window
    nblk = -(-L // blk)
    lp = nblk * blk
    q = jnp.pad(q, ((0, 0), (0, lp - L), (0, 0), (0, 0)))
    pad_kv = ((0, 0), (blk, lp - L + blk), (0, 0), (0, 0))
    kb = jnp.pad(k, pad_kv).reshape(n, nblk + 2, blk, hkv, hd)
    vb = jnp.pad(v, pad_kv).reshape(n, nblk + 2, blk, hkv, hd)
    kn = jnp.concatenate([kb[:, :-2], kb[:, 1:-1], kb[:, 2:]], axis=2)
    vn = jnp.concatenate([vb[:, :-2], vb[:, 1:-1], vb[:, 2:]], axis=2)
    qb = q.reshape(n, nblk, blk, hkv, g, hd)
    s = jnp.einsum('nbqhgd,nbkhd->nbhgqk', qb.astype(jnp.float32), kn.astype(jnp.float32)) * (hd ** -0.5)
    qpos = jnp.arange(nblk)[:, None] * blk + jnp.arange(blk)[None, :]
    kpos = (jnp.arange(nblk)[:, None] - 1) * blk + jnp.arange(3 * blk)[None, :]
    mask = ((jnp.abs(qpos[:, :, None] - kpos[:, None, :]) <= window)
            & (kpos[:, None, :] >= 0) & (kpos[:, None, :] < L))
    s = jnp.where(mask[None, :, None, None], s, NEG_INF)
    m = s.max(-1)
    if sink is not None:
        sk = sink.astype(jnp.float32).reshape(1, 1, hkv, g, 1)
        m = jnp.maximum(m, sk)
    p = jnp.exp(s - m[..., None])
    den = p.sum(-1)
    if sink is not None:
        den = den + jnp.exp(sk - m)
    o = jnp.einsum('nbhgqk,nbkhd->nbqhgd', p, vn.astype(jnp.float32))
    o = o / jnp.transpose(den, (0, 1, 4, 2, 3))[..., None]
    lse = jnp.transpose(m + jnp.log(den), (0, 1, 4, 2, 3)).reshape(n, lp, hq)[:, :L]
    o = o.reshape(n, lp, hq, hd)[:, :L]
    return o.astype(v.dtype), lse


def dilated_attention(q, k, v):
    B, S = q.shape[0], q.shape[1]
    hpg = B_HEADS_PER_GROUP
    outs, lses = [], []
    for gi, (w, d) in enumerate(B_GROUPS):
        lo, hi = gi * hpg, (gi + 1) * hpg
        m = S // d

        def fold(t):
            return (t[:, :, lo:hi].reshape(B, m, d, hpg, HEAD_DIM)
                    .transpose(0, 2, 1, 3, 4).reshape(B * d, m, hpg, HEAD_DIM))

        o, lse = banded_attention(fold(q), fold(k), fold(v), w // (2 * d))
        outs.append(o.reshape(B, d, m, hpg, HEAD_DIM).transpose(0, 2, 1, 3, 4).reshape(B, S, hpg, HEAD_DIM))
        lses.append(lse.reshape(B, d, m, hpg).transpose(0, 2, 1, 3).reshape(B, S, hpg))
    alpha = jax.nn.softmax(jnp.stack(lses), axis=0)
    o = jnp.einsum('gbsh,gbshd->bshd', alpha, jnp.stack(outs).astype(jnp.float32))
    return o.reshape(B, S, B_OUT_W).astype(q.dtype)


def encoder_layer(x, cos, sin, norm1, w_in, qn_a, kn_a, sink_a, qn_b, kn_b,
                  w_br_a, w_br_b, w_out, norm2, w_gate, w_up, w_down):
    B, S, _ = x.shape
    h = rms_norm(x, norm1)
    proj = h @ w_in
    qa, ka, va, qb, kb, vb, gates = jnp.split(proj, IN_SPLITS, axis=-1)
    qa = apply_rope(rms_norm(qa.reshape(B, S, A_Q_HEADS, HEAD_DIM), qn_a), cos, sin)
    ka = apply_rope(rms_norm(ka.reshape(B, S, A_KV_HEADS, HEAD_DIM), kn_a), cos, sin)
    va = va.reshape(B, S, A_KV_HEADS, HEAD_DIM)
    qb = apply_rope(rms_norm(qb.reshape(B, S, B_HEADS, HEAD_DIM), qn_b), cos, sin)
    kb = apply_rope(rms_norm(kb.reshape(B, S, B_HEADS, HEAD_DIM), kn_b), cos, sin)
    vb = vb.reshape(B, S, B_HEADS, HEAD_DIM)
    oa, _ = banded_attention(qa, ka, va, A_WINDOW, sink_a)
    ob = dilated_attention(qb, kb, vb)
    g_a, g_b = jnp.split(jax.nn.sigmoid(gates), 2, axis=-1)
    mix = g_a * (oa.reshape(B, S, A_Q_W) @ w_br_a) + g_b * (ob @ w_br_b)
    x = x + mix @ w_out
    h = rms_norm(x, norm2)
    x = x + (jax.nn.silu(h @ w_gate) * (h @ w_up)) @ w_down
    return x


def setup_inputs(seed: int = 0) -> dict:
    key = jax.random.key(seed)
    ks = jax.random.split(key, 20)
    f32 = jnp.float32

    def nrm(k, shape, scale):
        return jax.random.normal(k, shape, f32) * scale

    return {
        "x_prompt": nrm(ks[0], (BATCH, SEQ, D_MODEL), 1.0),
        "x_sample": nrm(ks[1], (DEC_BATCH, DEC_SEQ, D_MODEL), 1.0),
        "norm1": 1.0 + nrm(ks[2], (DEPTH, D_MODEL), 0.02),
        "w_in": nrm(ks[3], (DEPTH, D_MODEL, IN_WIDTH), D_MODEL ** -0.5),
        "qn_a": 1.0 + nrm(ks[4], (DEPTH, HEAD_DIM), 0.02),
        "kn_a": 1.0 + nrm(ks[5], (DEPTH, HEAD_DIM), 0.02),
        "sink_a": nrm(ks[6], (DEPTH, A_Q_HEADS), 0.5),
        "qn_b": 1.0 + nrm(ks[7], (DEPTH, HEAD_DIM), 0.02),
        "kn_b": 1.0 + nrm(ks[8], (DEPTH, HEAD_DIM), 0.02),
        "w_br_a": nrm(ks[9], (DEPTH, A_Q_W, D_MODEL), A_Q_W ** -0.5),
        "w_br_b": nrm(ks[10], (DEPTH, B_OUT_W, D_MODEL), B_OUT_W ** -0.5),
        "w_out": nrm(ks[11], (DEPTH, D_MODEL, D_MODEL), D_MODEL ** -0.5),
        "norm2": 1.0 + nrm(ks[12], (DEPTH, D_MODEL), 0.02),
        "w_gate": nrm(ks[13], (DEPTH, D_MODEL, D_FF), D_MODEL ** -0.5),
        "w_up": nrm(ks[14], (DEPTH, D_MODEL, D_FF), D_MODEL ** -0.5),
        "w_down": nrm(ks[15], (DEPTH, D_FF, D_MODEL), D_FF ** -0.5),
    }


def run_trunk(x, norm1, w_in, qn_a, kn_a, sink_a, qn_b, kn_b,
              w_br_a, w_br_b, w_out, norm2, w_gate, w_up, w_down):
    cos, sin = rope_tables(x.shape[1])
    for l in range(DEPTH):
        x = encoder_layer(x, cos, sin, norm1[l], w_in[l], qn_a[l], kn_a[l], sink_a[l],
                          qn_b[l], kn_b[l], w_br_a[l], w_br_b[l], w_out[l], norm2[l],
                          w_gate[l], w_up[l], w_down[l])
    return x


def reference(x_prompt, x_sample, norm1, w_in, qn_a, kn_a, sink_a, qn_b, kn_b,
              w_br_a, w_br_b, w_out, norm2, w_gate, w_up, w_down):
    y_prompt = run_trunk(x_prompt, norm1, w_in, qn_a, kn_a, sink_a, qn_b, kn_b,
                         w_br_a, w_br_b, w_out, norm2, w_gate, w_up, w_down)
    y_sample = run_trunk(x_sample, norm1, w_in, qn_a, kn_a, sink_a, qn_b, kn_b,
                         w_br_a, w_br_b, w_out, norm2, w_gate, w_up, w_down)
    return (y_prompt, y_sample)
```

```python
import functools

import numpy as np
import jax
import jax.numpy as jnp
from jax import lax
from jax.experimental import pallas as pl
from jax.experimental.pallas import tpu as pltpu

D_MODEL = 1024
HEAD_DIM = 64
HALF = HEAD_DIM // 2
A_Q_HEADS = 8
A_KV_HEADS = 2
A_WINDOW = 128
B_GROUPS = ((128, 1), (512, 4), (2048, 16))
B_HEADS_PER_GROUP = 4
B_HEADS = B_HEADS_PER_GROUP * len(B_GROUPS)
D_FF = 2816
ROPE_THETA = 10000.0
EPS = 1e-6
NEG_INF = -1e30

A_Q_W = A_Q_HEADS * HEAD_DIM
A_KV_W = A_KV_HEADS * HEAD_DIM
B_W = B_HEADS * HEAD_DIM
B_OUT_W = B_HEADS_PER_GROUP * HEAD_DIM
QKV_W = A_Q_W + 2 * A_KV_W + 3 * B_W
GROUP_W = 3 * B_OUT_W
A_W = A_Q_W + 2 * A_KV_W

LANES = 128
MXU_N = 256
VMEM_LIMIT = 56 * 1024 * 1024
BQ = 128
FF_CHUNK = 256
N_FF_CHUNKS = D_FF // FF_CHUNK

_BF = jnp.bfloat16
_F32 = jnp.float32


def _pair_cols(h0, h1, base):
    lane = np.arange(LANES)
    head = np.where((lane // HALF) % 2 == 0, h0, h1)
    dim = (lane // HEAD_DIM) * HALF + lane % HALF
    return base + head * HEAD_DIM + dim


def _qkv_columns():
    cols = []
    q_base, k_base, v_base = 0, A_Q_W, A_Q_W + A_KV_W
    for j in range(A_Q_HEADS // 2):
        cols.append(_pair_cols(j, j + A_Q_HEADS // 2, q_base))
    cols.append(_pair_cols(0, 1, k_base))
    cols.append(np.arange(v_base, v_base + A_KV_W))
    qb, kb, vb = A_W, A_W + B_W, A_W + 2 * B_W
    for g in range(len(B_GROUPS)):
        h = g * B_HEADS_PER_GROUP
        cols += [_pair_cols(h, h + 1, qb), _pair_cols(h + 2, h + 3, qb)]
        cols += [_pair_cols(h, h + 1, kb), _pair_cols(h + 2, h + 3, kb)]
        cols.append(np.arange(vb + h * HEAD_DIM, vb + (h + 4) * HEAD_DIM))
    return np.concatenate(cols)


_Q_A, _K_A, _Q_B, _K_B, _V = range(5)
_BLOCK_KINDS = ([_Q_A] * 4 + [_K_A, _V]) + ([_Q_B] * 2 + [_K_B] * 2 + [_V] * 2) * len(B_GROUPS)


def _oa_rows():
    rows = []
    for j in range(A_Q_HEADS // 2):
        rows.append(np.arange(j * HEAD_DIM, (j + 1) * HEAD_DIM))
        rows.append(np.arange((j + 4) * HEAD_DIM, (j + 5) * HEAD_DIM))
    return np.concatenate(rows)


def _interleave_gain(g):
    lane = np.arange(LANES)
    return g[(lane // HEAD_DIM) * HALF + lane % HALF].reshape(1, LANES)


def _rope_tables(seq_len):
    lane = np.arange(LANES)
    inv_freq = ROPE_THETA ** (-jnp.arange(0, HEAD_DIM, 2, dtype=_F32) / HEAD_DIM)
    ang = jnp.arange(seq_len, dtype=_F32)[:, None] * inv_freq[None, :]
    ang = ang[:, lane % HALF]
    sign = jnp.where(lane < HEAD_DIM, -1.0, 1.0).astype(_F32)
    return jnp.cos(ang), jnp.sin(ang) * sign[None, :]


def _same_head_matrix():
    lane = np.arange(LANES)
    head = (lane // HALF) % 2
    return jnp.asarray(head[:, None] == head[None, :], dtype=_BF)


def _rms_rows(x, gain):
    ms = jnp.mean(x * x, axis=-1, keepdims=True)
    return x * lax.rsqrt(ms + EPS) * gain


def _qkv_kernel(x_ref, n1_ref, w_ref, gains_ref, cos_ref, sin_ref, same_ref,
                oa_ref, o0_ref, o1_ref, o2_ref):
    h = _rms_rows(x_ref[...], n1_ref[...]).astype(_BF)
    cos = cos_ref[...]
    sin = sin_ref[...]
    same = same_ref[...]
    outs = (oa_ref, o0_ref, o1_ref, o2_ref)
    blocks_per_out = A_W // LANES
    for c in range(QKV_W // MXU_N):
        p2 = jnp.dot(h, w_ref[:, c * MXU_N:(c + 1) * MXU_N], preferred_element_type=_F32)
        for half in range(MXU_N // LANES):
            blk = c * (MXU_N // LANES) + half
            kind = _BLOCK_KINDS[blk]
            p = p2[:, half * LANES:(half + 1) * LANES]
            if kind != _V:
                sq = p * p
                hi = sq.astype(_BF)
                lo = (sq - hi.astype(_F32)).astype(_BF)
                ss = (jnp.dot(hi, same, preferred_element_type=_F32)
                      + jnp.dot(lo, same, preferred_element_type=_F32))
                gain = gains_ref[kind:kind + 1, :]
                t = p * lax.rsqrt(ss * (1.0 / HEAD_DIM) + EPS) * gain
                t = t * cos + pltpu.roll(t, HEAD_DIM, axis=1) * sin
                if kind in (_Q_A, _Q_B):
                    t = t * (HEAD_DIM ** -0.5)
                p = t
            out = outs[blk // blocks_per_out]
            lane0 = (blk % blocks_per_out) * LANES
            out[:, lane0:lane0 + LANES] = p.astype(_BF)


def _qkv_call(x2d, seq_len, n1, w_qkv, gains, cos, sin, same, tm):
    tokens = x2d.shape[0]
    tiles_per_seq = seq_len // tm
    const = lambda i: (0, 0)
    out_sds = jax.ShapeDtypeStruct((tokens, A_W), _BF)
    return pl.pallas_call(
        _qkv_kernel,
        grid=(tokens // tm,),
        in_specs=[
            pl.BlockSpec((tm, D_MODEL), lambda i: (i, 0)),
            pl.BlockSpec((1, D_MODEL), const),
            pl.BlockSpec((D_MODEL, QKV_W), const),
            pl.BlockSpec((4, LANES), const),
            pl.BlockSpec((tm, LANES), lambda i: (i % tiles_per_seq, 0)),
            pl.BlockSpec((tm, LANES), lambda i: (i % tiles_per_seq, 0)),
            pl.BlockSpec((LANES, LANES), const),
        ],
        out_specs=[pl.BlockSpec((tm, A_W), lambda i: (i, 0))] * 4,
        out_shape=[out_sds] * 4,
        compiler_params=pltpu.CompilerParams(
            dimension_semantics=("arbitrary",), vmem_limit_bytes=VMEM_LIMIT),
        name="qkv_proj",
    )(x2d, n1, w_qkv, gains, cos, sin, same)


def _attn_kernel(*refs, cfg):
    tq, halo, n_res, seq_rows = cfg["tq"], cfg["halo"], cfg["n_res"], cfg["seq_rows"]
    q_pairs, k_pairs, q_w, kv_w = cfg["q_pairs"], cfg["k_pairs"], cfg["q_w"], cfg["kv_w"]
    has_sink, has_state, final = cfg["has_sink"], cfg["has_state"], cfg["final"]
    slab = q_w + 2 * kv_w
    out_w = q_w
    bk = BQ + 2 * halo
    group = q_pairs // k_pairs

    it = iter(refs)
    cur_ref, prev_ref, next_ref, band_ref = next(it), next(it), next(it), next(it)
    sink_ref = next(it) if has_sink else None
    if has_state:
        o_in_ref, lse_in_ref = next(it), next(it)
    o_out_ref = next(it)
    lse_out_ref = None if final else next(it)
    kv_ext = next(it)

    t = pl.program_id(1)

    for r in range(n_res):
        src = slice(r * slab + q_w, (r + 1) * slab)
        dst = slice(r * 2 * kv_w, (r + 1) * 2 * kv_w)
        kv_ext[0:halo, dst] = prev_ref[:, src]
        kv_ext[halo:halo + tq, dst] = cur_ref[:, src]
        kv_ext[halo + tq:halo + tq + halo, dst] = next_ref[:, src]

    lane = lax.broadcasted_iota(jnp.int32, (1, LANES), 1)
    k_first = ((lane // HALF) % 2 == 0).astype(_F32).astype(_BF)
    k_second = ((lane // HALF) % 2 == 1).astype(_F32).astype(_BF)
    v_first = (lane < HEAD_DIM).astype(_F32).astype(_BF)
    v_second = (lane >= HEAD_DIM).astype(_F32).astype(_BF)
    first_half = lane < HEAD_DIM
    col = lax.broadcasted_iota(jnp.int32, (1, bk), 1)
    band = band_ref[...]

    def block(i, carry):
        row0 = pl.multiple_of(i * BQ, BQ)
        pos0 = t * tq + row0 - halo
        in_seq = (col + pos0 >= 0) & (col + pos0 < seq_rows)
        bias = band + jnp.where(in_seq, 0.0, NEG_INF)
        for r in range(n_res):
            for kp in range(k_pairs):
                k_lane = r * 2 * kv_w + kp * LANES
                v_lane = k_lane + kv_w
                kw = kv_ext[pl.ds(row0, bk), k_lane:k_lane + LANES]
                vw = kv_ext[pl.ds(row0, bk), v_lane:v_lane + LANES]
                k_cat = jnp.concatenate([kw * k_first, kw * k_second], axis=0)
                v_cat = jnp.concatenate([vw * v_first, vw * v_second], axis=0)
                q_lanes = [r * slab + (kp * group + g) * LANES for g in range(group)]
                q_cat = jnp.concatenate(
                    [cur_ref[pl.ds(row0, BQ), ql:ql + LANES] for ql in q_lanes], axis=0)
                s2 = lax.dot_general(q_cat, k_cat, (((1,), (1,)), ((), ())),
                                     preferred_element_type=_F32)
                p_rows, m_rows, l_rows = [], [], []
                for g in range(group):
                    ps, ms, ls = [], [], []
                    for hh in range(2):
                        s = s2[g * BQ:(g + 1) * BQ, hh * bk:(hh + 1) * bk] + bias
                        m = jnp.max(s, axis=-1, keepdims=True)
                        p = jnp.exp(s - m)
                        ls.append(jnp.sum(p, axis=-1, keepdims=True))
                        ms.append(m)
                        ps.append(p.astype(_BF))
                    p_rows.append(jnp.concatenate(ps, axis=1))
                    m_rows.append(jnp.where(first_half, ms[0], ms[1]))
                    l_rows.append(jnp.where(first_half, ls[0], ls[1]))
                p_cat = jnp.concatenate(p_rows, axis=0)
                pv_all = jnp.dot(p_cat, v_cat, preferred_element_type=_F32)
                for g in range(group):
                    o_lane = r * out_w + (kp * group + g) * LANES
                    pv = pv_all[g * BQ:(g + 1) * BQ, :]
                    m_blk, l_blk = m_rows[g], l_rows[g]
                    if has_sink or has_state:
                        if has_sink:
                            lse_in = sink_ref[:, o_lane:o_lane + LANES]
                        else:
                            lse_in = lse_in_ref[pl.ds(row0, BQ), o_lane:o_lane + LANES]
                        m_new = jnp.maximum(lse_in, m_blk)
                        a = jnp.exp(lse_in - m_new)
                        b = jnp.exp(m_blk - m_new)
                        l_new = a + b * l_blk
                        acc = b * pv
                        if has_state:
                            acc = acc + a * o_in_ref[pl.ds(row0, BQ), o_lane:o_lane + LANES]
                    else:
                        m_new, l_new, acc = m_blk, l_blk, pv
                    o_out_ref[pl.ds(row0, BQ), o_lane:o_lane + LANES] = (
                        acc / l_new).astype(o_out_ref.dtype)
                    if not final:
                        lse_out_ref[pl.ds(row0, BQ), o_lane:o_lane + LANES] = (
                            m_new + jnp.log(l_new))
        return carry

    lax.fori_loop(0, tq // BQ, block, 0)


def _band_bias(halo):
    bk = BQ + 2 * halo
    a = np.arange(BQ)[:, None]
    c = np.arange(bk)[None, :]
    return jnp.asarray(np.where((c >= a) & (c <= a + 2 * halo), 0.0, NEG_INF), dtype=_F32)


def _attn_call(qkv, *, dilation, halo, q_pairs, k_pairs, tq, n_res, final,
               sink=None, state=None, name):
    batch, seq, slab = qkv.shape
    rows = seq // dilation
    q_w, kv_w = q_pairs * LANES, k_pairs * LANES
    assert slab == q_w + 2 * kv_w and rows % tq == 0 and tq % halo == 0 and dilation % n_res == 0
    folded = qkv.reshape(batch, rows, dilation * slab)
    hb = tq // halo
    last_halo = rows // halo - 1
    grid = (batch, rows // tq, dilation // n_res)
    cur_map = lambda b, t, r: (b, t, r)
    in_specs = [
        pl.BlockSpec((None, tq, n_res * slab), cur_map),
        pl.BlockSpec((None, halo, n_res * slab),
                     lambda b, t, r: (b, jnp.maximum(t * hb - 1, 0), r)),
        pl.BlockSpec((None, halo, n_res * slab),
                     lambda b, t, r: (b, jnp.minimum((t + 1) * hb, last_halo), r)),
        pl.BlockSpec((BQ, BQ + 2 * halo), lambda b, t, r: (0, 0)),
    ]
    args = [folded, folded, folded, _band_bias(halo)]
    if sink is not None:
        in_specs.append(pl.BlockSpec((1, q_w), lambda b, t, r: (0, 0)))
        args.append(sink)
    out_block = pl.BlockSpec((None, tq, n_res * q_w), cur_map)
    if state is not None:
        in_specs += [out_block, out_block]
        args += [s.reshape(batch, rows, dilation * q_w) for s in state]
    out_shape = [jax.ShapeDtypeStruct((batch, rows, dilation * q_w), _BF if final else _F32)]
    if not final:
        out_shape.append(jax.ShapeDtypeStruct((batch, rows, dilation * q_w), _F32))
    cfg = dict(tq=tq, halo=halo, n_res=n_res, seq_rows=rows, q_pairs=q_pairs, k_pairs=k_pairs,
               q_w=q_w, kv_w=kv_w, has_sink=sink is not None, has_state=state is not None,
               final=final)
    outs = pl.pallas_call(
        functools.partial(_attn_kernel, cfg=cfg),
        grid=grid,
        in_specs=in_specs,
        out_specs=[out_block] * len(out_shape),
        out_shape=out_shape,
        scratch_shapes=[pltpu.VMEM((tq + 2 * halo, n_res * 2 * kv_w), _BF)],
        compiler_params=pltpu.CompilerParams(
            dimension_semantics=("arbitrary",) * 3, vmem_limit_bytes=VMEM_LIMIT),
        name=name,
    )(*args)
    return [o.reshape(batch, seq, q_w) for o in outs]


def _out_kernel(x_ref, oa_ref, ob_ref, n1_ref, wga_ref, wgb_ref, wa_ref, wb_ref, wo_ref,
                n2_ref, wg_ref, wu_ref, wd_ref, y_ref, mix_ref, acc_ref):
    x = x_ref[...]
    h1 = _rms_rows(x, n1_ref[...]).astype(_BF)
    oa = oa_ref[...]
    ob = ob_ref[...]
    for c in range(D_MODEL // MXU_N):
        cs = slice(c * MXU_N, (c + 1) * MXU_N)
        ga = jax.nn.sigmoid(jnp.dot(h1, wga_ref[:, cs], preferred_element_type=_F32))
        gb = jax.nn.sigmoid(jnp.dot(h1, wgb_ref[:, cs], preferred_element_type=_F32))
        ya = jnp.dot(oa, wa_ref[:, cs], preferred_element_type=_F32)
        yb = jnp.dot(ob, wb_ref[:, cs], preferred_element_type=_F32)
        mix_ref[:, cs] = (ga * ya + gb * yb).astype(_BF)
    x1 = x + jnp.dot(mix_ref[...], wo_ref[...], preferred_element_type=_F32)
    h2 = _rms_rows(x1, n2_ref[...]).astype(_BF)
    acc_ref[...] = x1

    def ff_chunk(c, carry):
        gate = jnp.dot(h2, wg_ref[c], preferred_element_type=_F32)
        up = jnp.dot(h2, wu_ref[c], preferred_element_type=_F32)
        act = (jax.nn.silu(gate) * up).astype(_BF)
        acc_ref[...] += jnp.dot(act, wd_ref[c], preferred_element_type=_F32)
        return carry

    lax.fori_loop(0, N_FF_CHUNKS, ff_chunk, 0)
    y_ref[...] = acc_ref[...]


def _out_call(x2d, oa, ob, n1, wga, wgb, wa, wb, wo, n2, wg, wu, wd, tm):
    tokens = x2d.shape[0]
    row = lambda i: (i, 0)
    const2 = lambda i: (0, 0)
    const3 = lambda i: (0, 0, 0)
    once = dict(pipeline_mode=pl.Buffered(1))
    return pl.pallas_call(
        _out_kernel,
        grid=(tokens // tm,),
        in_specs=[
            pl.BlockSpec((tm, D_MODEL), row),
            pl.BlockSpec((tm, A_Q_W), row),
            pl.BlockSpec((tm, B_OUT_W), row),
            pl.BlockSpec((1, D_MODEL), const2),
            pl.BlockSpec((D_MODEL, D_MODEL), const2, **once),
            pl.BlockSpec((D_MODEL, D_MODEL), const2, **once),
            pl.BlockSpec((A_Q_W, D_MODEL), const2, **once),
            pl.BlockSpec((B_OUT_W, D_MODEL), const2, **once),
            pl.BlockSpec((D_MODEL, D_MODEL), const2, **once),
            pl.BlockSpec((1, D_MODEL), const2),
            pl.BlockSpec((N_FF_CHUNKS, D_MODEL, FF_CHUNK), const3, **once),
            pl.BlockSpec((N_FF_CHUNKS, D_MODEL, FF_CHUNK), const3, **once),
            pl.BlockSpec((N_FF_CHUNKS, FF_CHUNK, D_MODEL), const3, **once),
        ],
        out_specs=pl.BlockSpec((tm, D_MODEL), row),
        out_shape=jax.ShapeDtypeStruct((tokens, D_MODEL), _F32),
        scratch_shapes=[pltpu.VMEM((tm, D_MODEL), _BF), pltpu.VMEM((tm, D_MODEL), _F32)],
        compiler_params=pltpu.CompilerParams(
            dimension_semantics=("arbitrary",), vmem_limit_bytes=VMEM_LIMIT),
        name="out_ffn",
    )(x2d, oa, ob, n1, wga, wgb, wa, wb, wo, n2, wg, wu, wd)


def _tiles(seq_len):
    tm = min(512, seq_len)
    plan = {"tm_qkv": tm, "tm_out": tm, "a_tq": min(512, seq_len)}
    groups = []
    for w, d in B_GROUPS:
        rows = seq_len // d
        tq = min(rows, max(BQ, 1024 // min(d, 4)))
        n_res = max(1, min(d, 1024 // tq))
        groups.append((tq, n_res))
    plan["groups"] = groups
    return plan


def _prepare_weights(norm1, w_in, qn_a, kn_a, sink_a, qn_b, kn_b, w_br_a, w_br_b, w_out,
                     norm2, w_gate, w_up, w_down):
    cols = _qkv_columns()
    w_in0 = w_in[0]
    gate0 = A_W + 3 * B_W
    gains = jnp.concatenate([
        qn_a[0][_interleave_gain(np.arange(HEAD_DIM))[0]][None, :],
        kn_a[0][_interleave_gain(np.arange(HEAD_DIM))[0]][None, :],
        qn_b[0][_interleave_gain(np.arange(HEAD_DIM))[0]][None, :],
        kn_b[0][_interleave_gain(np.arange(HEAD_DIM))[0]][None, :],
    ], axis=0).astype(_F32)
    sink = sink_a[0].astype(_F32)
    sink_pairs = jnp.concatenate(
        [jnp.concatenate([jnp.full((HEAD_DIM,), 1.0, _F32) * sink[j],
                          jnp.full((HEAD_DIM,), 1.0, _F32) * sink[j + A_Q_HEADS // 2]])
         for j in range(A_Q_HEADS // 2)]).reshape(1, A_Q_W)
    return dict(
        n1=norm1[0].reshape(1, D_MODEL).astype(_F32),
        w_qkv=w_in0[:, cols].astype(_BF),
        gains=gains,
        sink=sink_pairs,
        wga=w_in0[:, gate0:gate0 + D_MODEL].astype(_BF),
        wgb=w_in0[:, gate0 + D_MODEL:gate0 + 2 * D_MODEL].astype(_BF),
        wa=w_br_a[0][_oa_rows(), :].astype(_BF),
        wb=w_br_b[0].astype(_BF),
        wo=w_out[0].astype(_BF),
        n2=norm2[0].reshape(1, D_MODEL).astype(_F32),
        wg=w_gate[0].reshape(D_MODEL, N_FF_CHUNKS, FF_CHUNK).transpose(1, 0, 2).astype(_BF),
        wu=w_up[0].reshape(D_MODEL, N_FF_CHUNKS, FF_CHUNK).transpose(1, 0, 2).astype(_BF),
        wd=w_down[0].reshape(N_FF_CHUNKS, FF_CHUNK, D_MODEL).astype(_BF),
    )


def _trunk(x, wts, same):
    batch, seq, _ = x.shape
    plan = _tiles(seq)
    x2d = x.reshape(batch * seq, D_MODEL)
    cos, sin = _rope_tables(seq)
    qkv_a, qkv_0, qkv_1, qkv_2 = _qkv_call(
        x2d, seq, wts["n1"], wts["w_qkv"], wts["gains"], cos, sin, same, plan["tm_qkv"])

    (oa,) = _attn_call(
        qkv_a.reshape(batch, seq, A_W), dilation=1, halo=A_WINDOW, q_pairs=4, k_pairs=1,
        tq=plan["a_tq"], n_res=1, final=True, sink=wts["sink"], name="attn_window")

    state = None
    for g, (qkv_g, (w, d)) in enumerate(zip((qkv_0, qkv_1, qkv_2), B_GROUPS)):
        tq, n_res = plan["groups"][g]
        state = _attn_call(
            qkv_g.reshape(batch, seq, GROUP_W), dilation=d, halo=w // (2 * d), q_pairs=2,
            k_pairs=2, tq=tq, n_res=n_res, final=g == len(B_GROUPS) - 1, state=state,
            name=f"attn_dilated_{g}")
    (ob,) = state

    y = _out_call(x2d, oa.reshape(batch * seq, A_Q_W), ob.reshape(batch * seq, B_OUT_W),
                  wts["n1"], wts["wga"], wts["wgb"], wts["wa"], wts["wb"], wts["wo"],
                  wts["n2"], wts["wg"], wts["wu"], wts["wd"], plan["tm_out"])
    return y.reshape(batch, seq, D_MODEL)


def kernel(x_prompt, x_sample, norm1, w_in, qn_a, kn_a, sink_a, qn_b, kn_b, w_br_a, w_br_b,
           w_out, norm2, w_gate, w_up, w_down):
    wts = _prepare_weights(norm1, w_in, qn_a, kn_a, sink_a, qn_b, kn_b, w_br_a, w_br_b, w_out,
                           norm2, w_gate, w_up, w_down)
    same = _same_head_matrix()
    return (_trunk(x_prompt, wts, same), _trunk(x_sample, wts, same))
```

```python
import functools
import math

import numpy as np
import jax
import jax.numpy as jnp
from jax import lax
from jax.experimental import pallas as pl
from jax.experimental.pallas import tpu as pltpu

D_MODEL = 1024
HEAD_DIM = 64
HALF = HEAD_DIM // 2
A_Q_HEADS = 8
A_KV_HEADS = 2
A_WINDOW = 128
B_GROUPS = ((128, 1), (512, 4), (2048, 16))
B_HEADS_PER_GROUP = 4
B_HEADS = B_HEADS_PER_GROUP * len(B_GROUPS)
D_FF = 2816
ROPE_THETA = 10000.0
EPS = 1e-6
NEG_INF = -1e30

A_Q_W = A_Q_HEADS * HEAD_DIM
A_KV_W = A_KV_HEADS * HEAD_DIM
B_W = B_HEADS * HEAD_DIM
B_OUT_W = B_HEADS_PER_GROUP * HEAD_DIM
SLAB = A_Q_W + 2 * A_KV_W
QKV_W = SLAB * (1 + len(B_GROUPS))
assert SLAB == 3 * B_OUT_W

LANES = 128
MXU_N = 256
VMEM_LIMIT = 56 * 1024 * 1024
BQ = 128
B_HALO = 64
B_BK = BQ + 2 * B_HALO
A_BK = BQ + 2 * A_WINDOW
DIL_TILE = 2048
TOKEN_TILE = 512
A_TILE = 512
assert all(w // (2 * d) == B_HALO for w, d in B_GROUPS)
assert DIL_TILE // B_GROUPS[-1][1] == BQ
Q_SCALE = HEAD_DIM ** -0.5 * math.log2(math.e)

_BF = jnp.bfloat16
_F32 = jnp.float32
_NT = (((1,), (1,)), ((), ()))


def _pair_lane_dims():
    lane = np.arange(LANES)
    return (lane // HEAD_DIM) * HALF + lane % HALF


def _pair_cols(h0, h1, base):
    lane = np.arange(LANES)
    head = np.where((lane // HALF) % 2 == 0, h0, h1)
    return base + head * HEAD_DIM + _pair_lane_dims()


def _qkv_columns():
    cols = []
    q_base, k_base, v_base = 0, A_Q_W, A_Q_W + A_KV_W
    for j in range(A_Q_HEADS // 2):
        cols.append(_pair_cols(j, j + A_Q_HEADS // 2, q_base))
    cols.append(_pair_cols(0, 1, k_base))
    cols.append(np.arange(v_base, v_base + A_KV_W))
    qb, kb, vb = SLAB, SLAB + B_W, SLAB + 2 * B_W
    for g in range(len(B_GROUPS)):
        h = g * B_HEADS_PER_GROUP
        cols += [_pair_cols(h, h + 1, qb), _pair_cols(h + 2, h + 3, qb)]
        cols += [_pair_cols(h, h + 1, kb), _pair_cols(h + 2, h + 3, kb)]
        cols.append(np.arange(vb + h * HEAD_DIM, vb + (h + 4) * HEAD_DIM))
    return np.concatenate(cols)


_Q_A, _K_A, _Q_B, _K_B, _V = range(5)
_BLOCK_KINDS = ([_Q_A] * 4 + [_K_A, _V]) + ([_Q_B] * 2 + [_K_B] * 2 + [_V] * 2) * len(B_GROUPS)
_BLOCKS_PER_SLAB = SLAB // LANES


def _oa_rows():
    rows = []
    for j in range(A_Q_HEADS // 2):
        rows.append(np.arange(j * HEAD_DIM, (j + 1) * HEAD_DIM))
        rows.append(np.arange((j + 4) * HEAD_DIM, (j + 5) * HEAD_DIM))
    return np.concatenate(rows)


def _rope_tables(seq_len):
    lane = np.arange(LANES)
    inv_freq = ROPE_THETA ** (-jnp.arange(0, HEAD_DIM, 2, dtype=_F32) / HEAD_DIM)
    ang = jnp.arange(seq_len, dtype=_F32)[:, None] * inv_freq[None, :]
    ang = ang[:, lane % HALF]
    sign = jnp.where(lane < HEAD_DIM, -1.0, 1.0).astype(_F32)
    return jnp.cos(ang), jnp.sin(ang) * sign[None, :]


def _head_mean_matrix():
    lane = np.arange(MXU_N)
    head = lane // HALF % 2 + 2 * (lane // LANES)
    same = (head[:, None] == head[None, :]).astype(np.float32) / HEAD_DIM
    return jnp.asarray(np.concatenate([same, same], axis=0), dtype=_BF)


def _rms_rows(x, gain):
    ms = jnp.mean(x * x, axis=-1, keepdims=True)
    return x * lax.rsqrt(ms + EPS) * gain


def _qkv_kernel(x_ref, n1_ref, w_ref, gains_ref, cos_ref, sin_ref, mean_ref,
                oa_ref, o0_ref, o1_ref, o2_ref, stage_ref):
    h = _rms_rows(x_ref[...], n1_ref[...]).astype(_BF)
    cos = cos_ref[...]
    sin = sin_ref[...]
    blocks_per_chunk = MXU_N // LANES
    n_chunks = QKV_W // MXU_N

    def project(c):
        return jnp.dot(h, w_ref[:, c * MXU_N:(c + 1) * MXU_N], preferred_element_type=_F32)

    def unfold(g, out):
        d = B_GROUPS[g][1]
        for j in range(_BLOCKS_PER_SLAB):
            for r in range(d):
                rows = stage_ref[(g - 1) * _BLOCKS_PER_SLAB + j, pl.ds(r, tm // d, stride=d), :]
                out[r, :, j * LANES:(j + 1) * LANES] = rows.astype(_BF)

    tm = x_ref.shape[0]
    chunks_per_slab = SLAB // MXU_N
    order = [s * chunks_per_slab + k for s in (3, 2, 0, 1) for k in range(chunks_per_slab)]
    p2_next = project(order[0])
    for n, c in enumerate(order):
        kinds = _BLOCK_KINDS[c * blocks_per_chunk:(c + 1) * blocks_per_chunk]
        p2 = p2_next
        if n + 1 < n_chunks:
            p2_next = project(order[n + 1])
        if n == chunks_per_slab:
            unfold(2, o2_ref)
        if n == 2 * chunks_per_slab:
            unfold(1, o1_ref)
        if any(k != _V for k in kinds):
            sq = p2 * p2
            hi = sq.astype(_BF)
            lo = (sq - hi.astype(_F32)).astype(_BF)
            ms2 = jnp.dot(jnp.concatenate([hi, lo], axis=1), mean_ref[...],
                          preferred_element_type=_F32)
            inv2 = lax.rsqrt(ms2 + EPS)
        for half, kind in enumerate(kinds):
            blk = c * blocks_per_chunk + half
            p = p2[:, half * LANES:(half + 1) * LANES]
            if kind != _V:
                t = p * inv2[:, half * LANES:(half + 1) * LANES] * gains_ref[kind:kind + 1, :]
                p = t * cos + pltpu.roll(t, HEAD_DIM, axis=1) * sin
            slab, j = divmod(blk, _BLOCKS_PER_SLAB)
            if slab < 2:
                out = (oa_ref, o0_ref)[slab]
                out[:, j * LANES:(j + 1) * LANES] = p.astype(_BF)
            else:
                stage_ref[(slab - 2) * _BLOCKS_PER_SLAB + j] = p


def _qkv_call(x2d, seq_len, n1, w_qkv, gains, cos, sin, mean_mat):
    tokens = x2d.shape[0]
    tm = TOKEN_TILE
    tiles_per_seq = seq_len // tm
    sub = DIL_TILE // tm
    n_dil = tokens // DIL_TILE
    d1, d2 = B_GROUPS[1][1], B_GROUPS[2][1]
    const = lambda i: (0, 0)
    tok_sds = jax.ShapeDtypeStruct((tokens, SLAB), _BF)
    return pl.pallas_call(
        _qkv_kernel,
        grid=(tokens // tm,),
        in_specs=[
            pl.BlockSpec((tm, D_MODEL), lambda i: (i, 0)),
            pl.BlockSpec((1, D_MODEL), const),
            pl.BlockSpec((D_MODEL, QKV_W), const),
            pl.BlockSpec((4, LANES), const),
            pl.BlockSpec((tm, LANES), lambda i: (i % tiles_per_seq, 0)),
            pl.BlockSpec((tm, LANES), lambda i: (i % tiles_per_seq, 0)),
            pl.BlockSpec((2 * MXU_N, MXU_N), const),
        ],
        out_specs=[
            pl.BlockSpec((tm, SLAB), lambda i: (i, 0)),
            pl.BlockSpec((tm, SLAB), lambda i: (i, 0)),
            pl.BlockSpec((None, d1, tm // d1, SLAB), lambda i: (i // sub, 0, i % sub, 0)),
            pl.BlockSpec((None, d2, tm // d2, SLAB), lambda i: (i // sub, 0, i % sub, 0)),
        ],
        out_shape=[
            tok_sds, tok_sds,
            jax.ShapeDtypeStruct((n_dil, d1, DIL_TILE // d1, SLAB), _BF),
            jax.ShapeDtypeStruct((n_dil, d2, DIL_TILE // d2, SLAB), _BF),
        ],
        scratch_shapes=[pltpu.VMEM((2 * _BLOCKS_PER_SLAB, tm, LANES), _F32)],
        compiler_params=pltpu.CompilerParams(
            dimension_semantics=("arbitrary",), vmem_limit_bytes=VMEM_LIMIT),
        name="qkv_proj",
    )(x2d, n1, w_qkv, gains, cos, sin, mean_mat)


def _lane_masks():
    lane = lax.broadcasted_iota(jnp.int32, (1, LANES), 1)
    k0 = ((lane // HALF) % 2 == 0).astype(_F32).astype(_BF)
    k1 = ((lane // HALF) % 2 == 1).astype(_F32).astype(_BF)
    first_half = lane < HEAD_DIM
    return k0, k1, first_half


def _window_kernel(cur_ref, prev_ref, next_ref, tri_l_ref, tri_r_ref, sink_ref, o_ref, kv_ext,
                   *, seq_len):
    tq = cur_ref.shape[0]
    t = pl.program_id(1)
    kv = slice(A_Q_W, SLAB)
    kv_ext[0:A_WINDOW, :] = prev_ref[:, kv]
    kv_ext[A_WINDOW:A_WINDOW + tq, :] = cur_ref[:, kv]
    kv_ext[A_WINDOW + tq:, :] = next_ref[:, kv]

    k0, k1, first_half = _lane_masks()
    v0 = first_half.astype(_F32).astype(_BF)
    v1 = (~first_half).astype(_F32).astype(_BF)
    ones0 = jnp.broadcast_to(v0, (A_BK, LANES))
    ones1 = jnp.broadcast_to(v1, (A_BK, LANES))
    n_pairs = A_Q_W // LANES

    def block(i, carry):
        row0 = pl.multiple_of(i * BQ, BQ)
        start = t * tq + row0
        bias_l = tri_l_ref[...] + jnp.where(start - A_WINDOW < 0, NEG_INF, 0.0)
        bias_r = tri_r_ref[...] + jnp.where(start + BQ >= seq_len, NEG_INF, 0.0)
        kw = kv_ext[pl.ds(row0, A_BK), 0:LANES]
        vw = kv_ext[pl.ds(row0, A_BK), LANES:2 * LANES]
        k_cat = jnp.concatenate([kw * k0, kw * k1], axis=0)
        v_cat = jnp.concatenate([jnp.concatenate([vw * v0, ones0], axis=1),
                                 jnp.concatenate([vw * v1, ones1], axis=1)], axis=0)
        q_cat = jnp.concatenate(
            [cur_ref[pl.ds(row0, BQ), g * LANES:(g + 1) * LANES] for g in range(n_pairs)], axis=0)
        s2 = lax.dot_general(q_cat, k_cat, _NT, preferred_element_type=_F32)
        ps, ms = [], []
        for hh in range(2):
            base = hh * A_BK
            sl = s2[:, base:base + BQ] + bias_l
            sc = s2[:, base + BQ:base + 2 * BQ]
            sr = s2[:, base + 2 * BQ:base + 3 * BQ] + bias_r
            m = jnp.max(jnp.maximum(jnp.maximum(sl, sc), sr), axis=-1, keepdims=True)
            ms.append(m)
            ps += [jnp.exp2(sl - m).astype(_BF), jnp.exp2(sc - m).astype(_BF),
                   jnp.exp2(sr - m).astype(_BF)]
        res = jnp.dot(jnp.concatenate(ps, axis=1), v_cat, preferred_element_type=_F32)
        m_blk = jnp.where(first_half, ms[0], ms[1])
        for g in range(n_pairs):
            rows = slice(g * BQ, (g + 1) * BQ)
            sink = sink_ref[:, g * LANES:(g + 1) * LANES]
            m_new = jnp.maximum(sink, m_blk[rows])
            a = jnp.exp2(sink - m_new)
            b = jnp.exp2(m_blk[rows] - m_new)
            out = b * res[rows, 0:LANES] / (a + b * res[rows, LANES:2 * LANES])
            o_ref[pl.ds(row0, BQ), g * LANES:(g + 1) * LANES] = out.astype(_BF)
        return carry

    lax.fori_loop(0, tq // BQ, block, 0, unroll=2)


def _tri_bias(lower):
    a = np.arange(BQ)[:, None]
    c = np.arange(BQ)[None, :]
    keep = (c >= a) if lower else (c <= a)
    tile = np.where(keep, 0.0, NEG_INF).astype(np.float32)
    return jnp.asarray(np.tile(tile, (A_Q_W // LANES, 1)))


def _window_call(qkv, sink):
    batch, seq, _ = qkv.shape
    tq = A_TILE
    hb = tq // A_WINDOW
    last_halo = seq // A_WINDOW - 1
    const = lambda b, t: (0, 0)
    return pl.pallas_call(
        functools.partial(_window_kernel, seq_len=seq),
        grid=(batch, seq // tq),
        in_specs=[
            pl.BlockSpec((None, tq, SLAB), lambda b, t: (b, t, 0)),
            pl.BlockSpec((None, A_WINDOW, SLAB), lambda b, t: (b, jnp.maximum(t * hb - 1, 0), 0)),
            pl.BlockSpec((None, A_WINDOW, SLAB),
                         lambda b, t: (b, jnp.minimum((t + 1) * hb, last_halo), 0)),
            pl.BlockSpec((A_Q_W // LANES * BQ, BQ), const),
            pl.BlockSpec((A_Q_W // LANES * BQ, BQ), const),
            pl.BlockSpec((1, A_Q_W), const),
        ],
        out_specs=pl.BlockSpec((None, tq, A_Q_W), lambda b, t: (b, t, 0)),
        out_shape=jax.ShapeDtypeStruct((batch, seq, A_Q_W), _BF),
        scratch_shapes=[pltpu.VMEM((tq + 2 * A_WINDOW, 2 * A_KV_W), _BF)],
        compiler_params=pltpu.CompilerParams(
            dimension_semantics=("arbitrary",) * 2, vmem_limit_bytes=VMEM_LIMIT),
        name="attn_window",
    )(qkv, qkv, qkv, _tri_bias(True), _tri_bias(False), sink)


def _dilated_kernel(c0, p0, n0, c1, p1, n1, c2, p2, n2, band_ref, o_ref,
                    ext0, ext1, m_ref, l_ref, acc_ref, *, seq_len):
    t = pl.program_id(1)
    d1, d2 = B_GROUPS[1][1], B_GROUPS[2][1]
    kv = slice(B_OUT_W, SLAB)
    rows0, rows1 = DIL_TILE, DIL_TILE // d1
    ext0[0:B_HALO, :] = p0[:, kv]
    ext0[B_HALO:B_HALO + rows0, :] = c0[:, kv]
    ext0[B_HALO + rows0:, :] = n0[:, kv]
    for r in range(d1):
        ext1[r, 0:B_HALO, :] = p1[r, :, kv]
        ext1[r, B_HALO:B_HALO + rows1, :] = c1[r, :, kv]
        ext1[r, B_HALO + rows1:, :] = n1[r, :, kv]

    k0, k1, first_half = _lane_masks()
    col = lax.broadcasted_iota(jnp.int32, (1, B_BK), 1)
    ones = jnp.ones((B_BK, LANES), _BF)
    n_pairs = B_OUT_W // LANES

    def bias_for(pos0, limit):
        in_seq = (col + pos0 >= 0) & (col + pos0 < limit)
        return band_ref[...] + jnp.where(in_seq, 0.0, NEG_INF)

    def unit(q, kw, vw, bias):
        q2 = jnp.concatenate([q * k0, q * k1], axis=0)
        s = lax.dot_general(q2, kw, _NT, preferred_element_type=_F32) + bias
        m = jnp.max(s, axis=-1, keepdims=True)
        p = jnp.exp2(s - m).astype(_BF)
        res = jnp.dot(p, jnp.concatenate([vw, ones], axis=1), preferred_element_type=_F32)
        top, bot = res[0:BQ], res[BQ:2 * BQ]
        pv = jnp.where(first_half, top[:, 0:LANES], bot[:, 0:LANES])
        l = jnp.where(first_half, top[:, LANES:], bot[:, LANES:])
        return jnp.where(first_half, m[0:BQ], m[BQ:2 * BQ]), l, pv

    def merge(pair, rows, m_blk, l_blk, pv):
        m_old = m_ref[pair, rows, :]
        m_new = jnp.maximum(m_old, m_blk)
        a = jnp.exp2(m_old - m_new)
        b = jnp.exp2(m_blk - m_new)
        m_ref[pair, rows, :] = m_new
        l_ref[pair, rows, :] = a * l_ref[pair, rows, :] + b * l_blk
        acc_ref[pair, rows, :] = a * acc_ref[pair, rows, :] + b * pv

    def group0(i, carry):
        row0 = pl.multiple_of(i * BQ, BQ)
        bias = bias_for(t * DIL_TILE + row0 - B_HALO, seq_len)
        for pair in range(n_pairs):
            q = c0[pl.ds(row0, BQ), pair * LANES:(pair + 1) * LANES]
            kw = ext0[pl.ds(row0, B_BK), pair * LANES:(pair + 1) * LANES]
            vw = ext0[pl.ds(row0, B_BK), B_OUT_W + pair * LANES:B_OUT_W + (pair + 1) * LANES]
            m_blk, l_blk, pv = unit(q, kw, vw, bias)
            rows = pl.ds(row0, BQ)
            m_ref[pair, rows, :] = m_blk
            l_ref[pair, rows, :] = l_blk
            acc_ref[pair, rows, :] = pv
        return carry

    def group1(idx, carry):
        blocks = rows1 // BQ
        r = idx // blocks
        i = idx % blocks
        row0 = pl.multiple_of(i * BQ, BQ)
        bias = bias_for(t * rows1 + row0 - B_HALO, seq_len // d1)
        for pair in range(n_pairs):
            q = c1[r, pl.ds(row0, BQ), pair * LANES:(pair + 1) * LANES]
            kw = ext1[r, pl.ds(row0, B_BK), pair * LANES:(pair + 1) * LANES]
            vw = ext1[r, pl.ds(row0, B_BK), B_OUT_W + pair * LANES:B_OUT_W + (pair + 1) * LANES]
            m_blk, l_blk, pv = unit(q, kw, vw, bias)
            merge(pair, pl.ds(row0 * d1 + r, BQ, stride=d1), m_blk, l_blk, pv)
        return carry

    bias2 = bias_for(t * BQ - B_HALO, seq_len // d2)

    def group2(r, carry):
        bias = bias2
        for pair in range(n_pairs):
            q = c2[r, :, pair * LANES:(pair + 1) * LANES]
            k_l = B_OUT_W + pair * LANES
            v_l = 2 * B_OUT_W + pair * LANES
            kw = jnp.concatenate([p2[r, :, k_l:k_l + LANES], c2[r, :, k_l:k_l + LANES],
                                  n2[r, :, k_l:k_l + LANES]], axis=0)
            vw = jnp.concatenate([p2[r, :, v_l:v_l + LANES], c2[r, :, v_l:v_l + LANES],
                                  n2[r, :, v_l:v_l + LANES]], axis=0)
            m_blk, l_blk, pv = unit(q, kw, vw, bias)
            merge(pair, pl.ds(r, BQ, stride=d2), m_blk, l_blk, pv)
        return carry

    lax.fori_loop(0, rows0 // BQ, group0, 0, unroll=2)
    lax.fori_loop(0, DIL_TILE // BQ, group1, 0, unroll=2)
    lax.fori_loop(0, d2, group2, 0, unroll=2)

    def finish(i, carry):
        rows = pl.ds(pl.multiple_of(i * BQ, BQ), BQ)
        for pair in range(n_pairs):
            o_ref[rows, pair * LANES:(pair + 1) * LANES] = (
                acc_ref[pair, rows, :] / l_ref[pair, rows, :]).astype(_BF)
        return carry

    lax.fori_loop(0, DIL_TILE // BQ, finish, 0)


def _band_bias():
    a = np.arange(BQ)[:, None]
    c = np.arange(B_BK)[None, :]
    tile = np.where((c >= a) & (c <= a + 2 * B_HALO), 0.0, NEG_INF).astype(np.float32)
    return jnp.asarray(np.tile(tile, (2, 1)))


def _dilated_call(g0, g1, g2, batch, seq):
    n_t = seq // DIL_TILE
    d1, d2 = B_GROUPS[1][1], B_GROUPS[2][1]
    rows1, rows2 = DIL_TILE // d1, DIL_TILE // d2
    hb0 = DIL_TILE // B_HALO
    last0 = seq // B_HALO - 1
    prev_t = lambda t: jnp.maximum(t - 1, 0)
    next_t = lambda t: jnp.minimum(t + 1, n_t - 1)
    in_specs = [
        pl.BlockSpec((None, DIL_TILE, SLAB), lambda b, t: (b, t, 0)),
        pl.BlockSpec((None, B_HALO, SLAB), lambda b, t: (b, jnp.maximum(t * hb0 - 1, 0), 0)),
        pl.BlockSpec((None, B_HALO, SLAB), lambda b, t: (b, jnp.minimum((t + 1) * hb0, last0), 0)),
        pl.BlockSpec((None, None, d1, rows1, SLAB), lambda b, t: (b, t, 0, 0, 0)),
        pl.BlockSpec((None, None, d1, B_HALO, SLAB),
                     lambda b, t: (b, prev_t(t), 0, rows1 // B_HALO - 1, 0)),
        pl.BlockSpec((None, None, d1, B_HALO, SLAB), lambda b, t: (b, next_t(t), 0, 0, 0)),
        pl.BlockSpec((None, None, d2, rows2, SLAB), lambda b, t: (b, t, 0, 0, 0)),
        pl.BlockSpec((None, None, d2, B_HALO, SLAB),
                     lambda b, t: (b, prev_t(t), 0, rows2 // B_HALO - 1, 0)),
        pl.BlockSpec((None, None, d2, B_HALO, SLAB), lambda b, t: (b, next_t(t), 0, 0, 0)),
        pl.BlockSpec((2 * BQ, B_BK), lambda b, t: (0, 0)),
    ]
    n_pairs = B_OUT_W // LANES
    state = pltpu.VMEM((n_pairs, DIL_TILE, LANES), _F32)
    return pl.pallas_call(
        functools.partial(_dilated_kernel, seq_len=seq),
        grid=(batch, n_t),
        in_specs=in_specs,
        out_specs=pl.BlockSpec((None, DIL_TILE, B_OUT_W), lambda b, t: (b, t, 0)),
        out_shape=jax.ShapeDtypeStruct((batch, seq, B_OUT_W), _BF),
        scratch_shapes=[
            pltpu.VMEM((DIL_TILE + 2 * B_HALO, 2 * B_OUT_W), _BF),
            pltpu.VMEM((d1, rows1 + 2 * B_HALO, 2 * B_OUT_W), _BF),
            state, state, state,
        ],
        compiler_params=pltpu.CompilerParams(
            dimension_semantics=("arbitrary",) * 2, vmem_limit_bytes=VMEM_LIMIT),
        name="attn_dilated",
    )(g0, g0, g0, g1, g1, g1, g2, g2, g2, _band_bias())


def _out_kernel(x_ref, oa_ref, ob_ref, n1_ref, wga_ref, wgb_ref, wa_ref, wb_ref, wo_ref,
                n2_ref, wg_ref, wu_ref, wd_ref, y_ref, mix_ref, act_ref):
    x = x_ref[...]
    h1 = _rms_rows(x, n1_ref[...]).astype(_BF)
    oa = oa_ref[...]
    ob = ob_ref[...]
    for c in range(D_MODEL // MXU_N):
        cs = slice(c * MXU_N, (c + 1) * MXU_N)
        ga = jax.nn.sigmoid(jnp.dot(h1, wga_ref[:, cs], preferred_element_type=_F32))
        gb = jax.nn.sigmoid(jnp.dot(h1, wgb_ref[:, cs], preferred_element_type=_F32))
        ya = jnp.dot(oa, wa_ref[:, cs], preferred_element_type=_F32)
        yb = jnp.dot(ob, wb_ref[:, cs], preferred_element_type=_F32)
        mix_ref[:, cs] = (ga * ya + gb * yb).astype(_BF)
    x1 = x + jnp.dot(mix_ref[...], wo_ref[...], preferred_element_type=_F32)
    h2 = _rms_rows(x1, n2_ref[...]).astype(_BF)
    for c in range(D_FF // MXU_N):
        cs = slice(c * MXU_N, (c + 1) * MXU_N)
        gate = jnp.dot(h2, wg_ref[:, cs], preferred_element_type=_F32)
        up = jnp.dot(h2, wu_ref[:, cs], preferred_element_type=_F32)
        act_ref[:, cs] = (jax.nn.silu(gate) * up).astype(_BF)
    y_ref[...] = x1 + jnp.dot(act_ref[...], wd_ref[...], preferred_element_type=_F32)


def _out_call(x2d, oa, ob, n1, wga, wgb, wa, wb, wo, n2, wg, wu, wd):
    tokens = x2d.shape[0]
    tm = TOKEN_TILE
    row = lambda i: (i, 0)
    const = lambda i: (0, 0)
    once = dict(pipeline_mode=pl.Buffered(1))
    return pl.pallas_call(
        _out_kernel,
        grid=(tokens // tm,),
        in_specs=[
            pl.BlockSpec((tm, D_MODEL), row),
            pl.BlockSpec((tm, A_Q_W), row),
            pl.BlockSpec((tm, B_OUT_W), row),
            pl.BlockSpec((1, D_MODEL), const),
            pl.BlockSpec((D_MODEL, D_MODEL), const, **once),
            pl.BlockSpec((D_MODEL, D_MODEL), const, **once),
            pl.BlockSpec((A_Q_W, D_MODEL), const, **once),
            pl.BlockSpec((B_OUT_W, D_MODEL), const, **once),
            pl.BlockSpec((D_MODEL, D_MODEL), const, **once),
            pl.BlockSpec((1, D_MODEL), const),
            pl.BlockSpec((D_MODEL, D_FF), const, **once),
            pl.BlockSpec((D_MODEL, D_FF), const, **once),
            pl.BlockSpec((D_FF, D_MODEL), const, **once),
        ],
        out_specs=pl.BlockSpec((tm, D_MODEL), row),
        out_shape=jax.ShapeDtypeStruct((tokens, D_MODEL), _F32),
        scratch_shapes=[pltpu.VMEM((tm, D_MODEL), _BF), pltpu.VMEM((tm, D_FF), _BF)],
        compiler_params=pltpu.CompilerParams(
            dimension_semantics=("arbitrary",), vmem_limit_bytes=VMEM_LIMIT),
        name="out_ffn",
    )(x2d, oa, ob, n1, wga, wgb, wa, wb, wo, n2, wg, wu, wd)


def _prepare_weights(norm1, w_in, qn_a, kn_a, sink_a, qn_b, kn_b, w_br_a, w_br_b, w_out,
                     norm2, w_gate, w_up, w_down):
    w_in0 = w_in[0]
    gate0 = QKV_W
    dims = _pair_lane_dims()
    gains = jnp.stack([qn_a[0][dims] * Q_SCALE, kn_a[0][dims],
                       qn_b[0][dims] * Q_SCALE, kn_b[0][dims]]).astype(_F32)
    sink = sink_a[0].astype(_F32) * math.log2(math.e)
    half_heads = A_Q_HEADS // 2
    sink_pairs = jnp.stack([jnp.repeat(sink[:half_heads], HEAD_DIM).reshape(half_heads, HEAD_DIM),
                            jnp.repeat(sink[half_heads:], HEAD_DIM).reshape(half_heads, HEAD_DIM)],
                           axis=1).reshape(1, A_Q_W)
    return dict(
        n1=norm1[0].reshape(1, D_MODEL).astype(_F32),
        w_qkv=w_in0[:, _qkv_columns()].astype(_BF),
        gains=gains,
        sink=sink_pairs,
        wga=w_in0[:, gate0:gate0 + D_MODEL].astype(_BF),
        wgb=w_in0[:, gate0 + D_MODEL:gate0 + 2 * D_MODEL].astype(_BF),
        wa=w_br_a[0][_oa_rows(), :].astype(_BF),
        wb=w_br_b[0].astype(_BF),
        wo=w_out[0].astype(_BF),
        n2=norm2[0].reshape(1, D_MODEL).astype(_F32),
        wg=w_gate[0].astype(_BF),
        wu=w_up[0].astype(_BF),
        wd=w_down[0].astype(_BF),
    )


def _trunk(x, wts, mean_mat):
    batch, seq, _ = x.shape
    assert seq % DIL_TILE == 0
    tokens = batch * seq
    n_t = seq // DIL_TILE
    x2d = x.reshape(tokens, D_MODEL)
    cos, sin = _rope_tables(seq)
    qkv_a, g0, g1, g2 = _qkv_call(x2d, seq, wts["n1"], wts["w_qkv"], wts["gains"], cos, sin,
                                  mean_mat)
    oa = _window_call(qkv_a.reshape(batch, seq, SLAB), wts["sink"])
    ob = _dilated_call(g0.reshape(batch, seq, SLAB),
                       g1.reshape((batch, n_t) + g1.shape[1:]),
                       g2.reshape((batch, n_t) + g2.shape[1:]), batch, seq)
    y = _out_call(x2d, oa.reshape(tokens, A_Q_W), ob.reshape(tokens, B_OUT_W),
                  wts["n1"], wts["wga"], wts["wgb"], wts["wa"], wts["wb"], wts["wo"],
                  wts["n2"], wts["wg"], wts["wu"], wts["wd"])
    return y.reshape(batch, seq, D_MODEL)


def kernel(x_prompt, x_sample, norm1, w_in, qn_a, kn_a, sink_a, qn_b, kn_b, w_br_a, w_br_b,
           w_out, norm2, w_gate, w_up, w_down):
    wts = _prepare_weights(norm1, w_in, qn_a, kn_a, sink_a, qn_b, kn_b, w_br_a, w_br_b, w_out,
                           norm2, w_gate, w_up, w_down)
    mean_mat = _head_mean_matrix()
    return (_trunk(x_prompt, wts, mean_mat), _trunk(x_sample, wts, mean_mat))
```

```python
import functools
import math

import numpy as np
import jax
import jax.numpy as jnp
from jax import lax
from jax.experimental import pallas as pl
from jax.experimental.pallas import tpu as pltpu

D_MODEL = 1024
HEAD_DIM = 64
HALF = HEAD_DIM // 2
A_Q_HEADS = 8
A_KV_HEADS = 2
A_WINDOW = 128
B_GROUPS = ((128, 1), (512, 4), (2048, 16))
B_HEADS_PER_GROUP = 4
B_HEADS = B_HEADS_PER_GROUP * len(B_GROUPS)
D_FF = 2816
ROPE_THETA = 10000.0
EPS = 1e-6
NEG_INF = -1e30

A_Q_W = A_Q_HEADS * HEAD_DIM
A_KV_W = A_KV_HEADS * HEAD_DIM
B_W = B_HEADS * HEAD_DIM
B_OUT_W = B_HEADS_PER_GROUP * HEAD_DIM
PROJ_SLAB = A_Q_W + 2 * A_KV_W
QKV_W = PROJ_SLAB * (1 + len(B_GROUPS))
assert PROJ_SLAB == 3 * B_OUT_W

LANES = 128
MXU_N = 256
VMEM_LIMIT = 56 * 1024 * 1024
BQ = 128
B_HALO = 64
B_BK = BQ + 2 * B_HALO
A_BK = BQ + 2 * A_WINDOW
DIL_TILE = 2048
TOKEN_TILE = 512
A_TILE = 1024
DIL_UNROLL = 16
assert all(w // (2 * d) == B_HALO for w, d in B_GROUPS)
assert DIL_TILE // B_GROUPS[-1][1] == BQ
Q_SCALE = HEAD_DIM ** -0.5 * math.log2(math.e)

KV_W = 4 * LANES
SLAB = 8 * LANES
_NONE, _H0, _H1, _LO, _HI = range(5)
_A_EMIT = ([[(4 + j, _NONE)] for j in range(4)]
           + [[(0, _H0), (1, _H1)], [(2, _LO), (3, _HI)]])
_B_EMIT = ([[(4 + 2 * j, _H0), (5 + 2 * j, _H1)] for j in range(2)]
           + [[(j, _NONE)] for j in range(2)] + [[(2 + j, _NONE)] for j in range(2)])

_BF = jnp.bfloat16
_F32 = jnp.float32
_NT = (((1,), (1,)), ((), ()))


def _pair_lane_dims():
    lane = np.arange(LANES)
    return (lane // HEAD_DIM) * HALF + lane % HALF


def _pair_cols(h0, h1, base):
    lane = np.arange(LANES)
    head = np.where((lane // HALF) % 2 == 0, h0, h1)
    return base + head * HEAD_DIM + _pair_lane_dims()


def _qkv_columns():
    cols = []
    q_base, k_base, v_base = 0, A_Q_W, A_Q_W + A_KV_W
    for j in range(A_Q_HEADS // 2):
        cols.append(_pair_cols(j, j + A_Q_HEADS // 2, q_base))
    cols.append(_pair_cols(0, 1, k_base))
    cols.append(np.arange(v_base, v_base + A_KV_W))
    qb, kb, vb = PROJ_SLAB, PROJ_SLAB + B_W, PROJ_SLAB + 2 * B_W
    for g in range(len(B_GROUPS)):
        h = g * B_HEADS_PER_GROUP
        cols += [_pair_cols(h, h + 1, qb), _pair_cols(h + 2, h + 3, qb)]
        cols += [_pair_cols(h, h + 1, kb), _pair_cols(h + 2, h + 3, kb)]
        cols.append(np.arange(vb + h * HEAD_DIM, vb + (h + 4) * HEAD_DIM))
    return np.concatenate(cols)


_Q_A, _K_A, _Q_B, _K_B, _V = range(5)
_BLOCK_KINDS = ([_Q_A] * 4 + [_K_A, _V]) + ([_Q_B] * 2 + [_K_B] * 2 + [_V] * 2) * len(B_GROUPS)
_PROJ_BLOCKS = PROJ_SLAB // LANES


def _oa_rows():
    rows = []
    for j in range(A_Q_HEADS // 2):
        rows.append(np.arange(j * HEAD_DIM, (j + 1) * HEAD_DIM))
        rows.append(np.arange((j + 4) * HEAD_DIM, (j + 5) * HEAD_DIM))
    return np.concatenate(rows)


def _rope_tables(seq_len):
    lane = np.arange(LANES)
    inv_freq = ROPE_THETA ** (-jnp.arange(0, HEAD_DIM, 2, dtype=_F32) / HEAD_DIM)
    ang = jnp.arange(seq_len, dtype=_F32)[:, None] * inv_freq[None, :]
    ang = ang[:, lane % HALF]
    sign = jnp.where(lane < HEAD_DIM, -1.0, 1.0).astype(_F32)
    return jnp.cos(ang), jnp.sin(ang) * sign[None, :]


def _head_mean_matrix():
    lane = np.arange(MXU_N)
    head = lane // HALF % 2 + 2 * (lane // LANES)
    same = (head[:, None] == head[None, :]).astype(np.float32) / HEAD_DIM
    return jnp.asarray(np.concatenate([same, same], axis=0), dtype=_BF)


def _one_hot_rows(n_rows):
    return jnp.asarray(np.arange(n_rows)[:, None] % BQ == np.arange(BQ)[None, :], dtype=_BF)


def _bias_variants(halo, copies):
    bk = BQ + 2 * halo
    c = np.arange(bk)[:, None]
    a = np.arange(BQ)[None, :]
    band = (c >= a) & (c <= a + 2 * halo)
    out = []
    for first, last in ((0, 0), (1, 0), (0, 1), (1, 1)):
        keep = band & ((c >= halo) | (not first)) & ((c < bk - halo) | (not last))
        out.append(np.tile(np.where(keep, 0.0, NEG_INF).astype(np.float32), (copies, 1)))
    return jnp.asarray(np.stack(out), dtype=_BF)


def _lane_rows(dtype):
    lane = lax.broadcasted_iota(jnp.int32, (1, LANES), 1)
    head1 = (lane // HALF) % 2 == 1
    hi = lane >= HEAD_DIM
    as_row = lambda m: m.astype(_F32).astype(dtype)
    return [None, as_row(~head1), as_row(head1), as_row(~hi), as_row(hi)]


def _rms_rows(x, gain):
    ms = jnp.mean(x * x, axis=-1, keepdims=True)
    return x * lax.rsqrt(ms + EPS) * gain


def _qkv_kernel(x_ref, n1_ref, w_ref, gains_ref, cos_ref, sin_ref, mean_ref,
                oa_ref, o0_ref, o1_ref, o2_ref, stage_ref):
    h = _rms_rows(x_ref[...], n1_ref[...]).astype(_BF)
    cos = cos_ref[...]
    sin = sin_ref[...]
    masks = _lane_rows(_F32)
    blocks_per_chunk = MXU_N // LANES
    n_chunks = QKV_W // MXU_N
    tm = x_ref.shape[0]

    def project(c):
        return jnp.dot(h, w_ref[:, c * MXU_N:(c + 1) * MXU_N], preferred_element_type=_F32)

    def masked(p, mask):
        return (p if mask == _NONE else p * masks[mask]).astype(_BF)

    def unfold(g, out):
        d = B_GROUPS[g][1]
        for j in range(_PROJ_BLOCKS):
            for r in range(d):
                rows = stage_ref[(g - 1) * _PROJ_BLOCKS + j, pl.ds(r, tm // d, stride=d), :]
                for dest, mask in _B_EMIT[j]:
                    out[r, :, dest * LANES:(dest + 1) * LANES] = masked(rows, mask)

    chunks_per_slab = PROJ_SLAB // MXU_N
    order = [s * chunks_per_slab + k for s in (3, 2, 0, 1) for k in range(chunks_per_slab)]
    p2_next = project(order[0])
    for n, c in enumerate(order):
        kinds = _BLOCK_KINDS[c * blocks_per_chunk:(c + 1) * blocks_per_chunk]
        p2 = p2_next
        if n + 1 < n_chunks:
            p2_next = project(order[n + 1])
        if n == chunks_per_slab:
            unfold(2, o2_ref)
        if n == 2 * chunks_per_slab:
            unfold(1, o1_ref)
        if any(k != _V for k in kinds):
            sq = p2 * p2
            hi = sq.astype(_BF)
            lo = (sq - hi.astype(_F32)).astype(_BF)
            ms2 = jnp.dot(jnp.concatenate([hi, lo], axis=1), mean_ref[...],
                          preferred_element_type=_F32)
            inv2 = lax.rsqrt(ms2 + EPS)
        for half, kind in enumerate(kinds):
            blk = c * blocks_per_chunk + half
            p = p2[:, half * LANES:(half + 1) * LANES]
            if kind != _V:
                t = p * inv2[:, half * LANES:(half + 1) * LANES] * gains_ref[kind:kind + 1, :]
                p = t * cos + pltpu.roll(t, HEAD_DIM, axis=1) * sin
            slab, j = divmod(blk, _PROJ_BLOCKS)
            if slab < 2:
                out, emit = ((oa_ref, _A_EMIT), (o0_ref, _B_EMIT))[slab]
                for dest, mask in emit[j]:
                    out[:, dest * LANES:(dest + 1) * LANES] = masked(p, mask)
            else:
                stage_ref[(slab - 2) * _PROJ_BLOCKS + j] = p


def _qkv_call(x2d, seq_len, n1, w_qkv, gains, cos, sin, mean_mat):
    tokens = x2d.shape[0]
    tm = TOKEN_TILE
    tiles_per_seq = seq_len // tm
    sub = DIL_TILE // tm
    n_dil = tokens // DIL_TILE
    d1, d2 = B_GROUPS[1][1], B_GROUPS[2][1]
    const = lambda i: (0, 0)
    tok_sds = jax.ShapeDtypeStruct((tokens, SLAB), _BF)
    return pl.pallas_call(
        _qkv_kernel,
        grid=(tokens // tm,),
        in_specs=[
            pl.BlockSpec((tm, D_MODEL), lambda i: (i, 0)),
            pl.BlockSpec((1, D_MODEL), const),
            pl.BlockSpec((D_MODEL, QKV_W), const),
            pl.BlockSpec((4, LANES), const),
            pl.BlockSpec((tm, LANES), lambda i: (i % tiles_per_seq, 0)),
            pl.BlockSpec((tm, LANES), lambda i: (i % tiles_per_seq, 0)),
            pl.BlockSpec((2 * MXU_N, MXU_N), const),
        ],
        out_specs=[
            pl.BlockSpec((tm, SLAB), lambda i: (i, 0)),
            pl.BlockSpec((tm, SLAB), lambda i: (i, 0)),
            pl.BlockSpec((None, d1, tm // d1, SLAB), lambda i: (i // sub, 0, i % sub, 0)),
            pl.BlockSpec((None, d2, tm // d2, SLAB), lambda i: (i // sub, 0, i % sub, 0)),
        ],
        out_shape=[
            tok_sds, tok_sds,
            jax.ShapeDtypeStruct((n_dil, d1, DIL_TILE // d1, SLAB), _BF),
            jax.ShapeDtypeStruct((n_dil, d2, DIL_TILE // d2, SLAB), _BF),
        ],
        scratch_shapes=[pltpu.VMEM((2 * _PROJ_BLOCKS, tm, LANES), _F32)],
        compiler_params=pltpu.CompilerParams(
            dimension_semantics=("arbitrary",), vmem_limit_bytes=VMEM_LIMIT),
        name="qkv_proj",
    )(x2d, n1, w_qkv, gains, cos, sin, mean_mat)


def _pick_bias(bias_ref, first, last):
    return jnp.where(first, jnp.where(last, bias_ref[3], bias_ref[1]),
                     jnp.where(last, bias_ref[2], bias_ref[0]))


def _window_kernel(cur_ref, prev_ref, next_ref, bias_ref, onehot_ref, sink_ref, o_ref, kv_ext,
                   *, seq_len):
    tq = cur_ref.shape[0]
    t = pl.program_id(1)
    kv_ext[0:A_WINDOW, :] = prev_ref[...]
    kv_ext[A_WINDOW:A_WINDOW + tq, :] = cur_ref[:, 0:KV_W]
    kv_ext[A_WINDOW + tq:, :] = next_ref[...]

    masks = _lane_rows(_BF)
    first_half = lax.broadcasted_iota(jnp.int32, (1, LANES), 1) < HEAD_DIM
    ones_lo = jnp.broadcast_to(masks[_LO], (A_BK, LANES))
    ones_hi = jnp.broadcast_to(masks[_HI], (A_BK, LANES))
    onehot = onehot_ref[...]
    n_pairs = A_Q_W // LANES

    def block(i, carry):
        row0 = pl.multiple_of(i * BQ, BQ)
        start = t * tq + row0
        bias = _pick_bias(bias_ref, start - A_WINDOW < 0, start + BQ >= seq_len)
        win = pl.ds(row0, A_BK)
        k_cat = jnp.concatenate([kv_ext[win, 0:LANES], kv_ext[win, LANES:2 * LANES]], axis=0)
        k_ext = jnp.concatenate([k_cat, bias], axis=1)
        v_cat = jnp.concatenate(
            [jnp.concatenate([kv_ext[win, 2 * LANES:3 * LANES], ones_lo], axis=1),
             jnp.concatenate([kv_ext[win, 3 * LANES:4 * LANES], ones_hi], axis=1)], axis=0)
        q_cat = jnp.concatenate(
            [cur_ref[pl.ds(row0, BQ), KV_W + g * LANES:KV_W + (g + 1) * LANES]
             for g in range(n_pairs)], axis=0)
        q_ext = jnp.concatenate([q_cat, onehot], axis=1)
        s2 = lax.dot_general(q_ext, k_ext, _NT, preferred_element_type=_F32)
        ps, ms = [], []
        for hh in range(2):
            s = s2[:, hh * A_BK:(hh + 1) * A_BK]
            m = jnp.max(s, axis=-1, keepdims=True)
            ms.append(m)
            ps.append(jnp.exp2(s - m).astype(_BF))
        res = jnp.dot(jnp.concatenate(ps, axis=1), v_cat, preferred_element_type=_F32)
        m_blk = jnp.where(first_half, ms[0], ms[1])
        for g in range(n_pairs):
            rows = slice(g * BQ, (g + 1) * BQ)
            sink = sink_ref[:, g * LANES:(g + 1) * LANES]
            m_new = jnp.maximum(sink, m_blk[rows])
            a = jnp.exp2(sink - m_new)
            b = jnp.exp2(m_blk[rows] - m_new)
            out = b * res[rows, 0:LANES] / (a + b * res[rows, LANES:2 * LANES])
            o_ref[pl.ds(row0, BQ), g * LANES:(g + 1) * LANES] = out.astype(_BF)
        return carry

    lax.fori_loop(0, tq // BQ, block, 0, unroll=True)


def _window_call(qkv, sink):
    batch, seq, _ = qkv.shape
    tq = A_TILE
    hb = tq // A_WINDOW
    last_halo = seq // A_WINDOW - 1
    n_pairs = A_Q_W // LANES
    const2 = lambda b, t: (0, 0)
    return pl.pallas_call(
        functools.partial(_window_kernel, seq_len=seq),
        grid=(batch, seq // tq),
        in_specs=[
            pl.BlockSpec((None, tq, SLAB), lambda b, t: (b, t, 0)),
            pl.BlockSpec((None, A_WINDOW, KV_W), lambda b, t: (b, jnp.maximum(t * hb - 1, 0), 0)),
            pl.BlockSpec((None, A_WINDOW, KV_W),
                         lambda b, t: (b, jnp.minimum((t + 1) * hb, last_halo), 0)),
            pl.BlockSpec((4, 2 * A_BK, BQ), lambda b, t: (0, 0, 0)),
            pl.BlockSpec((n_pairs * BQ, BQ), const2),
            pl.BlockSpec((1, A_Q_W), const2),
        ],
        out_specs=pl.BlockSpec((None, tq, A_Q_W), lambda b, t: (b, t, 0)),
        out_shape=jax.ShapeDtypeStruct((batch, seq, A_Q_W), _BF),
        scratch_shapes=[pltpu.VMEM((tq + 2 * A_WINDOW, KV_W), _BF)],
        compiler_params=pltpu.CompilerParams(
            dimension_semantics=("arbitrary",) * 2, vmem_limit_bytes=VMEM_LIMIT),
        name="attn_window",
    )(qkv, qkv, qkv, _bias_variants(A_WINDOW, 2), _one_hot_rows(n_pairs * BQ), sink)


def _dilated_kernel(c0, p0, n0, c1, p1, n1, c2, p2, n2, bias_ref, onehot_ref, o_ref,
                    ext0, ext1, lse_ref, acc_ref, *, seq_len):
    t = pl.program_id(1)
    d1, d2 = B_GROUPS[1][1], B_GROUPS[2][1]
    rows0, rows1 = DIL_TILE, DIL_TILE // d1
    n_pairs = B_OUT_W // LANES
    masks = _lane_rows(_BF)
    k_w = n_pairs * LANES

    def spread(src):
        parts = [src[:, 0:k_w]]
        for pair in range(n_pairs):
            v = src[:, k_w + pair * LANES:k_w + (pair + 1) * LANES]
            parts += [v * masks[_LO], v * masks[_HI]]
        return jnp.concatenate(parts, axis=1)

    ext0[0:B_HALO, :] = spread(p0[...])
    ext0[B_HALO:B_HALO + rows0, :] = spread(c0[:, 0:KV_W])
    ext0[B_HALO + rows0:, :] = spread(n0[...])
    for r in range(d1):
        ext1[r, 0:B_HALO, :] = spread(p1[r])
        ext1[r, B_HALO:B_HALO + rows1, :] = spread(c1[r, :, 0:KV_W])
        ext1[r, B_HALO + rows1:, :] = spread(n1[r])

    first_half = lax.broadcasted_iota(jnp.int32, (1, LANES), 1) < HEAD_DIM
    ones_lo = jnp.broadcast_to(masks[_LO], (B_BK, LANES))
    ones_hi = jnp.broadcast_to(masks[_HI], (B_BK, LANES))
    onehot = onehot_ref[...]

    def bias_for(pos0, limit):
        return _pick_bias(bias_ref, pos0 < 0, pos0 + B_BK > limit)

    def unit(q2, kw, v_lo, v_hi, bias):
        s = lax.dot_general(jnp.concatenate([q2, onehot], axis=1),
                            jnp.concatenate([kw, bias], axis=1), _NT,
                            preferred_element_type=_F32)
        m = jnp.max(s, axis=-1, keepdims=True)
        p = jnp.exp2(s - m).astype(_BF)
        p_cat = jnp.concatenate([p[0:BQ], p[BQ:2 * BQ]], axis=1)
        v_cat = jnp.concatenate([jnp.concatenate([v_lo, ones_lo], axis=1),
                                 jnp.concatenate([v_hi, ones_hi], axis=1)], axis=0)
        res = jnp.dot(p_cat, v_cat, preferred_element_type=_F32)
        return jnp.where(first_half, m[0:BQ], m[BQ:2 * BQ]), res[:, LANES:], res[:, 0:LANES]

    def stacked_q(block):
        return jnp.concatenate([block[:, 0:LANES], block[:, LANES:2 * LANES]], axis=0)

    def merge(pair, rows, m_blk, l_blk, pv, last):
        lse_old = lse_ref[pair, rows, :]
        m_new = jnp.maximum(lse_old, m_blk)
        a = jnp.exp2(lse_old - m_new)
        b = jnp.exp2(m_blk - m_new)
        l_new = a + b * l_blk
        acc_ref[pair, rows, :] = (a * acc_ref[pair, rows, :] + b * pv) / l_new
        if not last:
            lse_ref[pair, rows, :] = m_new + jnp.log2(l_new)

    def ext_unit(q_block, ext_window, pair, bias):
        kw = ext_window[:, pair * LANES:(pair + 1) * LANES]
        v0 = k_w + 2 * pair * LANES
        return unit(stacked_q(q_block), kw, ext_window[:, v0:v0 + LANES],
                    ext_window[:, v0 + LANES:v0 + 2 * LANES], bias)

    def group0(i, carry):
        row0 = pl.multiple_of(i * BQ, BQ)
        bias = bias_for(t * DIL_TILE + row0 - B_HALO, seq_len)
        for pair in range(n_pairs):
            q0 = KV_W + 2 * pair * LANES
            m_blk, l_blk, pv = ext_unit(c0[pl.ds(row0, BQ), q0:q0 + 2 * LANES],
                                        ext0[pl.ds(row0, B_BK), :], pair, bias)
            rows = pl.ds(row0, BQ)
            lse_ref[pair, rows, :] = m_blk + jnp.log2(l_blk)
            acc_ref[pair, rows, :] = pv / l_blk
        return carry

    def group1(idx, carry):
        blocks = rows1 // BQ
        r = idx // blocks
        i = idx % blocks
        row0 = pl.multiple_of(i * BQ, BQ)
        bias = bias_for(t * rows1 + row0 - B_HALO, seq_len // d1)
        for pair in range(n_pairs):
            q0 = KV_W + 2 * pair * LANES
            m_blk, l_blk, pv = ext_unit(c1[r, pl.ds(row0, BQ), q0:q0 + 2 * LANES],
                                        ext1[r, pl.ds(row0, B_BK), :], pair, bias)
            merge(pair, pl.ds(row0 * d1 + r, BQ, stride=d1), m_blk, l_blk, pv, last=False)
        return carry

    bias2 = bias_for(t * BQ - B_HALO, seq_len // d2)

    def group2(r, carry):
        for pair in range(n_pairs):
            q0 = KV_W + 2 * pair * LANES
            k_l = pair * LANES
            v_l = k_w + pair * LANES
            kw = jnp.concatenate([p2[r, :, k_l:k_l + LANES], c2[r, :, k_l:k_l + LANES],
                                  n2[r, :, k_l:k_l + LANES]], axis=0)
            vw = jnp.concatenate([p2[r, :, v_l:v_l + LANES], c2[r, :, v_l:v_l + LANES],
                                  n2[r, :, v_l:v_l + LANES]], axis=0)
            m_blk, l_blk, pv = unit(stacked_q(c2[r, :, q0:q0 + 2 * LANES]), kw,
                                    vw * masks[_LO], vw * masks[_HI], bias2)
            merge(pair, pl.ds(r, BQ, stride=d2), m_blk, l_blk, pv, last=True)
        return carry

    lax.fori_loop(0, rows0 // BQ, group0, 0, unroll=DIL_UNROLL)
    lax.fori_loop(0, DIL_TILE // BQ, group1, 0, unroll=DIL_UNROLL)
    lax.fori_loop(0, d2, group2, 0, unroll=DIL_UNROLL)

    def finish(i, carry):
        rows = pl.ds(pl.multiple_of(i * BQ, BQ), BQ)
        for pair in range(n_pairs):
            o_ref[rows, pair * LANES:(pair + 1) * LANES] = acc_ref[pair, rows, :].astype(_BF)
        return carry

    lax.fori_loop(0, DIL_TILE // BQ, finish, 0)


def _dilated_call(g0, g1, g2, batch, seq):
    n_t = seq // DIL_TILE
    d1, d2 = B_GROUPS[1][1], B_GROUPS[2][1]
    rows1, rows2 = DIL_TILE // d1, DIL_TILE // d2
    hb0 = DIL_TILE // B_HALO
    last0 = seq // B_HALO - 1
    prev_t = lambda t: jnp.maximum(t - 1, 0)
    next_t = lambda t: jnp.minimum(t + 1, n_t - 1)
    in_specs = [
        pl.BlockSpec((None, DIL_TILE, SLAB), lambda b, t: (b, t, 0)),
        pl.BlockSpec((None, B_HALO, KV_W), lambda b, t: (b, jnp.maximum(t * hb0 - 1, 0), 0)),
        pl.BlockSpec((None, B_HALO, KV_W), lambda b, t: (b, jnp.minimum((t + 1) * hb0, last0), 0)),
        pl.BlockSpec((None, None, d1, rows1, SLAB), lambda b, t: (b, t, 0, 0, 0)),
        pl.BlockSpec((None, None, d1, B_HALO, KV_W),
                     lambda b, t: (b, prev_t(t), 0, rows1 // B_HALO - 1, 0)),
        pl.BlockSpec((None, None, d1, B_HALO, KV_W), lambda b, t: (b, next_t(t), 0, 0, 0)),
        pl.BlockSpec((None, None, d2, rows2, SLAB), lambda b, t: (b, t, 0, 0, 0)),
        pl.BlockSpec((None, None, d2, B_HALO, KV_W),
                     lambda b, t: (b, prev_t(t), 0, rows2 // B_HALO - 1, 0)),
        pl.BlockSpec((None, None, d2, B_HALO, KV_W), lambda b, t: (b, next_t(t), 0, 0, 0)),
        pl.BlockSpec((4, B_BK, BQ), lambda b, t: (0, 0, 0)),
        pl.BlockSpec((2 * BQ, BQ), lambda b, t: (0, 0)),
    ]
    n_pairs = B_OUT_W // LANES
    ext_w = n_pairs * 3 * LANES
    state = pltpu.VMEM((n_pairs, DIL_TILE, LANES), _F32)
    return pl.pallas_call(
        functools.partial(_dilated_kernel, seq_len=seq),
        grid=(batch, n_t),
        in_specs=in_specs,
        out_specs=pl.BlockSpec((None, DIL_TILE, B_OUT_W), lambda b, t: (b, t, 0)),
        out_shape=jax.ShapeDtypeStruct((batch, seq, B_OUT_W), _BF),
        scratch_shapes=[
            pltpu.VMEM((DIL_TILE + 2 * B_HALO, ext_w), _BF),
            pltpu.VMEM((d1, rows1 + 2 * B_HALO, ext_w), _BF),
            state, state,
        ],
        compiler_params=pltpu.CompilerParams(
            dimension_semantics=("arbitrary",) * 2, vmem_limit_bytes=VMEM_LIMIT),
        name="attn_dilated",
    )(g0, g0, g0, g1, g1, g1, g2, g2, g2, _bias_variants(B_HALO, 1), _one_hot_rows(2 * BQ))


def _out_kernel(x_ref, oa_ref, ob_ref, n1_ref, wga_ref, wgb_ref, wa_ref, wb_ref, wo_ref,
                n2_ref, wg_ref, wu_ref, wd_ref, y_ref, mix_ref, act_ref):
    x = x_ref[...]
    h1 = _rms_rows(x, n1_ref[...]).astype(_BF)
    oa = oa_ref[...]
    ob = ob_ref[...]
    for c in range(D_MODEL // MXU_N):
        cs = slice(c * MXU_N, (c + 1) * MXU_N)
        ga = jax.nn.sigmoid(jnp.dot(h1, wga_ref[:, cs], preferred_element_type=_F32))
        gb = jax.nn.sigmoid(jnp.dot(h1, wgb_ref[:, cs], preferred_element_type=_F32))
        ya = jnp.dot(oa, wa_ref[:, cs], preferred_element_type=_F32)
        yb = jnp.dot(ob, wb_ref[:, cs], preferred_element_type=_F32)
        mix_ref[:, cs] = (ga * ya + gb * yb).astype(_BF)
    x1 = x + jnp.dot(mix_ref[...], wo_ref[...], preferred_element_type=_F32)
    h2 = _rms_rows(x1, n2_ref[...]).astype(_BF)
    for c in range(D_FF // MXU_N):
        cs = slice(c * MXU_N, (c + 1) * MXU_N)
        gate = jnp.dot(h2, wg_ref[:, cs], preferred_element_type=_F32)
        up = jnp.dot(h2, wu_ref[:, cs], preferred_element_type=_F32)
        act_ref[:, cs] = (jax.nn.silu(gate) * up).astype(_BF)
    y_ref[...] = x1 + jnp.dot(act_ref[...], wd_ref[...], preferred_element_type=_F32)


def _out_call(x2d, oa, ob, n1, wga, wgb, wa, wb, wo, n2, wg, wu, wd):
    tokens = x2d.shape[0]
    tm = TOKEN_TILE
    row = lambda i: (i, 0)
    const = lambda i: (0, 0)
    once = dict(pipeline_mode=pl.Buffered(1))
    return pl.pallas_call(
        _out_kernel,
        grid=(tokens // tm,),
        in_specs=[
            pl.BlockSpec((tm, D_MODEL), row),
            pl.BlockSpec((tm, A_Q_W), row),
            pl.BlockSpec((tm, B_OUT_W), row),
            pl.BlockSpec((1, D_MODEL), const),
            pl.BlockSpec((D_MODEL, D_MODEL), const, **once),
            pl.BlockSpec((D_MODEL, D_MODEL), const, **once),
            pl.BlockSpec((A_Q_W, D_MODEL), const, **once),
            pl.BlockSpec((B_OUT_W, D_MODEL), const, **once),
            pl.BlockSpec((D_MODEL, D_MODEL), const, **once),
            pl.BlockSpec((1, D_MODEL), const),
            pl.BlockSpec((D_MODEL, D_FF), const, **once),
            pl.BlockSpec((D_MODEL, D_FF), const, **once),
            pl.BlockSpec((D_FF, D_MODEL), const, **once),
        ],
        out_specs=pl.BlockSpec((tm, D_MODEL), row),
        out_shape=jax.ShapeDtypeStruct((tokens, D_MODEL), _F32),
        scratch_shapes=[pltpu.VMEM((tm, D_MODEL), _BF), pltpu.VMEM((tm, D_FF), _BF)],
        compiler_params=pltpu.CompilerParams(
            dimension_semantics=("arbitrary",), vmem_limit_bytes=VMEM_LIMIT),
        name="out_ffn",
    )(x2d, oa, ob, n1, wga, wgb, wa, wb, wo, n2, wg, wu, wd)


def _prepare_weights(norm1, w_in, qn_a, kn_a, sink_a, qn_b, kn_b, w_br_a, w_br_b, w_out,
                     norm2, w_gate, w_up, w_down):
    w_in0 = w_in[0]
    gate0 = QKV_W
    dims = _pair_lane_dims()
    gains = jnp.stack([qn_a[0][dims] * Q_SCALE, kn_a[0][dims],
                       qn_b[0][dims] * Q_SCALE, kn_b[0][dims]]).astype(_F32)
    sink = sink_a[0].astype(_F32) * math.log2(math.e)
    half_heads = A_Q_HEADS // 2
    sink_pairs = jnp.stack([jnp.repeat(sink[:half_heads], HEAD_DIM).reshape(half_heads, HEAD_DIM),
                            jnp.repeat(sink[half_heads:], HEAD_DIM).reshape(half_heads, HEAD_DIM)],
                           axis=1).reshape(1, A_Q_W)
    return dict(
        n1=norm1[0].reshape(1, D_MODEL).astype(_F32),
        w_qkv=w_in0[:, _qkv_columns()].astype(_BF),
        gains=gains,
        sink=sink_pairs,
        wga=w_in0[:, gate0:gate0 + D_MODEL].astype(_BF),
        wgb=w_in0[:, gate0 + D_MODEL:gate0 + 2 * D_MODEL].astype(_BF),
        wa=w_br_a[0][_oa_rows(), :].astype(_BF),
        wb=w_br_b[0].astype(_BF),
        wo=w_out[0].astype(_BF),
        n2=norm2[0].reshape(1, D_MODEL).astype(_F32),
        wg=w_gate[0].astype(_BF),
        wu=w_up[0].astype(_BF),
        wd=w_down[0].astype(_BF),
    )


def _trunk(x, wts, mean_mat):
    batch, seq, _ = x.shape
    assert seq % DIL_TILE == 0 and seq >= 2 * BQ
    tokens = batch * seq
    n_t = seq // DIL_TILE
    x2d = x.reshape(tokens, D_MODEL)
    cos, sin = _rope_tables(seq)
    qkv_a, g0, g1, g2 = _qkv_call(x2d, seq, wts["n1"], wts["w_qkv"], wts["gains"], cos, sin,
                                  mean_mat)
    oa = _window_call(qkv_a.reshape(batch, seq, SLAB), wts["sink"])
    ob = _dilated_call(g0.reshape(batch, seq, SLAB),
                       g1.reshape((batch, n_t) + g1.shape[1:]),
                       g2.reshape((batch, n_t) + g2.shape[1:]), batch, seq)
    y = _out_call(x2d, oa.reshape(tokens, A_Q_W), ob.reshape(tokens, B_OUT_W),
                  wts["n1"], wts["wga"], wts["wgb"], wts["wa"], wts["wb"], wts["wo"],
                  wts["n2"], wts["wg"], wts["wu"], wts["wd"])
    return y.reshape(batch, seq, D_MODEL)


def kernel(x_prompt, x_sample, norm1, w_in, qn_a, kn_a, sink_a, qn_b, kn_b, w_br_a, w_br_b,
           w_out, norm2, w_gate, w_up, w_down):
    wts = _prepare_weights(norm1, w_in, qn_a, kn_a, sink_a, qn_b, kn_b, w_br_a, w_br_b, w_out,
                           norm2, w_gate, w_up, w_down)
    mean_mat = _head_mean_matrix()
    return (_trunk(x_prompt, wts, mean_mat), _trunk(x_sample, wts, mean_mat))
```

```python
import functools
import math

import numpy as np
import jax
import jax.numpy as jnp
from jax import lax
from jax.experimental import pallas as pl
from jax.experimental.pallas import tpu as pltpu

D_MODEL = 1024
HEAD_DIM = 64
HALF = HEAD_DIM // 2
A_Q_HEADS = 8
A_KV_HEADS = 2
A_WINDOW = 128
B_GROUPS = ((128, 1), (512, 4), (2048, 16))
B_HEADS_PER_GROUP = 4
B_HEADS = B_HEADS_PER_GROUP * len(B_GROUPS)
D_FF = 2816
ROPE_THETA = 10000.0
EPS = 1e-6
NEG_INF = -1e30

A_Q_W = A_Q_HEADS * HEAD_DIM
A_KV_W = A_KV_HEADS * HEAD_DIM
B_W = B_HEADS * HEAD_DIM
B_OUT_W = B_HEADS_PER_GROUP * HEAD_DIM
PROJ_SLAB = A_Q_W + 2 * A_KV_W
QKV_W = PROJ_SLAB * (1 + len(B_GROUPS))
assert PROJ_SLAB == 3 * B_OUT_W

LANES = 128
MXU_N = 256
VMEM_LIMIT = 56 * 1024 * 1024
BQ = 128
B_HALO = 64
B_BK = BQ + 2 * B_HALO
A_BK = BQ + 2 * A_WINDOW
DIL_TILE = 2048
TOKEN_TILE = 512
QKV_TILE = 512
A_TILE = 1024
DIL_UNROLL = 16
assert all(w // (2 * d) == B_HALO for w, d in B_GROUPS)
assert DIL_TILE // B_GROUPS[-1][1] == BQ
Q_SCALE = HEAD_DIM ** -0.5 * math.log2(math.e)

KV_W = 4 * LANES
SLAB = 8 * LANES
_NONE, _H0, _H1, _LO, _HI = range(5)
_A_EMIT = ([[(4 + j, _NONE)] for j in range(4)]
           + [[(0, _H0), (1, _H1)], [(2, _LO), (3, _HI)]])
_B_EMIT = ([[(4 + 2 * j, _H0), (5 + 2 * j, _H1)] for j in range(2)]
           + [[(j, _NONE)] for j in range(2)] + [[(2 + j, _NONE)] for j in range(2)])

_BF = jnp.bfloat16
_F32 = jnp.float32
_NT = (((1,), (1,)), ((), ()))


def _pair_lane_dims():
    lane = np.arange(LANES)
    return (lane // HEAD_DIM) * HALF + lane % HALF


def _pair_cols(h0, h1, base):
    lane = np.arange(LANES)
    head = np.where((lane // HALF) % 2 == 0, h0, h1)
    return base + head * HEAD_DIM + _pair_lane_dims()


def _qkv_columns():
    cols = []
    q_base, k_base, v_base = 0, A_Q_W, A_Q_W + A_KV_W
    for j in range(A_Q_HEADS // 2):
        cols.append(_pair_cols(j, j + A_Q_HEADS // 2, q_base))
    cols.append(_pair_cols(0, 1, k_base))
    cols.append(np.arange(v_base, v_base + A_KV_W))
    qb, kb, vb = PROJ_SLAB, PROJ_SLAB + B_W, PROJ_SLAB + 2 * B_W
    for g in range(len(B_GROUPS)):
        h = g * B_HEADS_PER_GROUP
        cols += [_pair_cols(h, h + 1, qb), _pair_cols(h + 2, h + 3, qb)]
        cols += [_pair_cols(h, h + 1, kb), _pair_cols(h + 2, h + 3, kb)]
        cols.append(np.arange(vb + h * HEAD_DIM, vb + (h + 4) * HEAD_DIM))
    return np.concatenate(cols)


_Q_A, _K_A, _Q_B, _K_B, _V = range(5)
_BLOCK_KINDS = ([_Q_A] * 4 + [_K_A, _V]) + ([_Q_B] * 2 + [_K_B] * 2 + [_V] * 2) * len(B_GROUPS)
_PROJ_BLOCKS = PROJ_SLAB // LANES


def _oa_rows():
    rows = []
    for j in range(A_Q_HEADS // 2):
        rows.append(np.arange(j * HEAD_DIM, (j + 1) * HEAD_DIM))
        rows.append(np.arange((j + 4) * HEAD_DIM, (j + 5) * HEAD_DIM))
    return np.concatenate(rows)


def _rope_tables(seq_len):
    inv_freq = ROPE_THETA ** (-jnp.arange(0, HEAD_DIM, 2, dtype=_F32) / HEAD_DIM)
    ang = jnp.arange(seq_len, dtype=_F32)[:, None] * inv_freq[None, :]
    cos, sin = jnp.cos(ang), jnp.sin(ang)
    return (jnp.concatenate([cos] * 4, axis=1), jnp.concatenate([-sin, -sin, sin, sin], axis=1))


def _head_mean_matrix():
    lane = np.arange(MXU_N)
    head = lane // HALF % 2 + 2 * (lane // LANES)
    same = (head[:, None] == head[None, :]).astype(np.float32) / HEAD_DIM
    return jnp.asarray(same, dtype=_BF)


def _one_hot_rows(n_rows):
    return jnp.asarray(np.arange(n_rows)[:, None] % BQ == np.arange(BQ)[None, :], dtype=_BF)


def _bias_variants(halo, copies):
    bk = BQ + 2 * halo
    c = np.arange(bk)[:, None]
    a = np.arange(BQ)[None, :]
    band = (c >= a) & (c <= a + 2 * halo)
    out = []
    for first, last in ((0, 0), (1, 0), (0, 1), (1, 1)):
        keep = band & ((c >= halo) | (not first)) & ((c < bk - halo) | (not last))
        out.append(np.tile(np.where(keep, 0.0, NEG_INF).astype(np.float32), (copies, 1)))
    return jnp.asarray(np.stack(out), dtype=_BF)


def _lane_rows(dtype):
    lane = lax.broadcasted_iota(jnp.int32, (1, LANES), 1)
    head1 = (lane // HALF) % 2 == 1
    hi = lane >= HEAD_DIM
    as_row = lambda m: m.astype(_F32).astype(dtype)
    return [None, as_row(~head1), as_row(head1), as_row(~hi), as_row(hi)]


def _rms_rows(x, gain):
    ms = jnp.mean(x * x, axis=-1, keepdims=True)
    return x * lax.rsqrt(ms + EPS) * gain


def _qkv_kernel(x_ref, n1_ref, w_ref, gains_ref, cos_ref, sin_ref, mean_ref,
                oa_ref, o0_ref, o1_ref, o2_ref, stage_ref):
    h = _rms_rows(x_ref[...], n1_ref[...]).astype(_BF)
    cos = cos_ref[...]
    sin = sin_ref[...]
    masks = _lane_rows(_F32)
    blocks_per_chunk = MXU_N // LANES
    n_chunks = QKV_W // MXU_N
    tm = x_ref.shape[0]

    def project(c):
        return jnp.dot(h, w_ref[:, c * MXU_N:(c + 1) * MXU_N], preferred_element_type=_F32)

    def masked(p, mask):
        return (p if mask == _NONE else p * masks[mask]).astype(_BF)

    def unfold(g, out):
        d = B_GROUPS[g][1]
        for j in range(_PROJ_BLOCKS):
            for r in range(d):
                rows = stage_ref[(g - 1) * _PROJ_BLOCKS + j, pl.ds(r, tm // d, stride=d), :]
                for dest, mask in _B_EMIT[j]:
                    out[r, :, dest * LANES:(dest + 1) * LANES] = masked(rows, mask)

    chunks_per_slab = PROJ_SLAB // MXU_N
    order = [s * chunks_per_slab + k for s in (3, 2, 0, 1) for k in range(chunks_per_slab)]
    p2_next = project(order[0])
    for n, c in enumerate(order):
        kinds = _BLOCK_KINDS[c * blocks_per_chunk:(c + 1) * blocks_per_chunk]
        p2 = p2_next
        if n + 1 < n_chunks:
            p2_next = project(order[n + 1])
        if n == chunks_per_slab:
            unfold(2, o2_ref)
        if n == 2 * chunks_per_slab:
            unfold(1, o1_ref)
        if any(k != _V for k in kinds):
            ms2 = jnp.dot((p2 * p2).astype(_BF), mean_ref[...], preferred_element_type=_F32)
            inv2 = lax.rsqrt(ms2 + EPS)
        for half, kind in enumerate(kinds):
            blk = c * blocks_per_chunk + half
            p = p2[:, half * LANES:(half + 1) * LANES]
            if kind != _V:
                t = p * inv2[:, half * LANES:(half + 1) * LANES] * gains_ref[kind:kind + 1, :]
                p = t * cos + pltpu.roll(t, HEAD_DIM, axis=1) * sin
            slab, j = divmod(blk, _PROJ_BLOCKS)
            if slab < 2:
                out, emit = ((oa_ref, _A_EMIT), (o0_ref, _B_EMIT))[slab]
                for dest, mask in emit[j]:
                    out[:, dest * LANES:(dest + 1) * LANES] = masked(p, mask)
            else:
                stage_ref[(slab - 2) * _PROJ_BLOCKS + j] = p


def _qkv_call(x2d, seq_len, n1, w_qkv, gains, cos, sin, mean_mat):
    tokens = x2d.shape[0]
    tm = QKV_TILE
    tiles_per_seq = seq_len // tm
    sub = DIL_TILE // tm
    n_dil = tokens // DIL_TILE
    d1, d2 = B_GROUPS[1][1], B_GROUPS[2][1]
    const = lambda i: (0, 0)
    tok_sds = jax.ShapeDtypeStruct((tokens, SLAB), _BF)
    return pl.pallas_call(
        _qkv_kernel,
        grid=(tokens // tm,),
        in_specs=[
            pl.BlockSpec((tm, D_MODEL), lambda i: (i, 0)),
            pl.BlockSpec((1, D_MODEL), const),
            pl.BlockSpec((D_MODEL, QKV_W), const, pipeline_mode=pl.Buffered(1)),
            pl.BlockSpec((4, LANES), const),
            pl.BlockSpec((tm, LANES), lambda i: (i % tiles_per_seq, 0)),
            pl.BlockSpec((tm, LANES), lambda i: (i % tiles_per_seq, 0)),
            pl.BlockSpec((MXU_N, MXU_N), const),
        ],
        out_specs=[
            pl.BlockSpec((tm, SLAB), lambda i: (i, 0)),
            pl.BlockSpec((tm, SLAB), lambda i: (i, 0)),
            pl.BlockSpec((None, d1, tm // d1, SLAB), lambda i: (i // sub, 0, i % sub, 0)),
            pl.BlockSpec((None, d2, tm // d2, SLAB), lambda i: (i // sub, 0, i % sub, 0)),
        ],
        out_shape=[
            tok_sds, tok_sds,
            jax.ShapeDtypeStruct((n_dil, d1, DIL_TILE // d1, SLAB), _BF),
            jax.ShapeDtypeStruct((n_dil, d2, DIL_TILE // d2, SLAB), _BF),
        ],
        scratch_shapes=[pltpu.VMEM((2 * _PROJ_BLOCKS, tm, LANES), _F32)],
        compiler_params=pltpu.CompilerParams(
            dimension_semantics=("arbitrary",), vmem_limit_bytes=VMEM_LIMIT),
        name="qkv_proj",
    )(x2d, n1, w_qkv, gains, cos, sin, mean_mat)


def _pick_bias(bias_ref, first, last):
    return jnp.where(first, jnp.where(last, bias_ref[3], bias_ref[1]),
                     jnp.where(last, bias_ref[2], bias_ref[0]))


def _window_kernel(cur_ref, prev_ref, next_ref, bias_ref, onehot_ref, sink_ref, o_ref, kv_ext,
                   *, seq_len):
    tq = cur_ref.shape[0]
    t = pl.program_id(1)
    kv_ext[0:A_WINDOW, :] = prev_ref[...]
    kv_ext[A_WINDOW:A_WINDOW + tq, :] = cur_ref[:, 0:KV_W]
    kv_ext[A_WINDOW + tq:, :] = next_ref[...]

    masks = _lane_rows(_BF)
    first_half = lax.broadcasted_iota(jnp.int32, (1, LANES), 1) < HEAD_DIM
    ones_lo = jnp.broadcast_to(masks[_LO], (A_BK, LANES))
    ones_hi = jnp.broadcast_to(masks[_HI], (A_BK, LANES))
    onehot = onehot_ref[...]
    n_pairs = A_Q_W // LANES

    def block(i, carry):
        row0 = pl.multiple_of(i * BQ, BQ)
        start = t * tq + row0
        bias = _pick_bias(bias_ref, start - A_WINDOW < 0, start + BQ >= seq_len)
        win = pl.ds(row0, A_BK)
        k_cat = jnp.concatenate([kv_ext[win, 0:LANES], kv_ext[win, LANES:2 * LANES]], axis=0)
        k_ext = jnp.concatenate([k_cat, bias], axis=1)
        v_cat = jnp.concatenate(
            [jnp.concatenate([kv_ext[win, 2 * LANES:3 * LANES], ones_lo], axis=1),
             jnp.concatenate([kv_ext[win, 3 * LANES:4 * LANES], ones_hi], axis=1)], axis=0)
        q_cat = jnp.concatenate(
            [cur_ref[pl.ds(row0, BQ), KV_W + g * LANES:KV_W + (g + 1) * LANES]
             for g in range(n_pairs)], axis=0)
        q_ext = jnp.concatenate([q_cat, onehot], axis=1)
        s2 = lax.dot_general(q_ext, k_ext, _NT, preferred_element_type=_F32)
        ps, ms = [], []
        for hh in range(2):
            s = s2[:, hh * A_BK:(hh + 1) * A_BK]
            m = jnp.max(s, axis=-1, keepdims=True)
            ms.append(m)
            ps.append(jnp.exp2(s - m).astype(_BF))
        res = jnp.dot(jnp.concatenate(ps, axis=1), v_cat, preferred_element_type=_F32)
        m_blk = jnp.where(first_half, ms[0], ms[1])
        for g in range(n_pairs):
            rows = slice(g * BQ, (g + 1) * BQ)
            sink = sink_ref[:, g * LANES:(g + 1) * LANES]
            m_new = jnp.maximum(sink, m_blk[rows])
            a = jnp.exp2(sink - m_new)
            b = jnp.exp2(m_blk[rows] - m_new)
            out = b * res[rows, 0:LANES] / (a + b * res[rows, LANES:2 * LANES])
            o_ref[pl.ds(row0, BQ), g * LANES:(g + 1) * LANES] = out.astype(_BF)
        return carry

    lax.fori_loop(0, tq // BQ, block, 0, unroll=True)


def _window_call(qkv, sink):
    batch, seq, _ = qkv.shape
    tq = A_TILE
    hb = tq // A_WINDOW
    last_halo = seq // A_WINDOW - 1
    n_pairs = A_Q_W // LANES
    const2 = lambda b, t: (0, 0)
    return pl.pallas_call(
        functools.partial(_window_kernel, seq_len=seq),
        grid=(batch, seq // tq),
        in_specs=[
            pl.BlockSpec((None, tq, SLAB), lambda b, t: (b, t, 0)),
            pl.BlockSpec((None, A_WINDOW, KV_W), lambda b, t: (b, jnp.maximum(t * hb - 1, 0), 0)),
            pl.BlockSpec((None, A_WINDOW, KV_W),
                         lambda b, t: (b, jnp.minimum((t + 1) * hb, last_halo), 0)),
            pl.BlockSpec((4, 2 * A_BK, BQ), lambda b, t: (0, 0, 0)),
            pl.BlockSpec((n_pairs * BQ, BQ), const2),
            pl.BlockSpec((1, A_Q_W), const2),
        ],
        out_specs=pl.BlockSpec((None, tq, A_Q_W), lambda b, t: (b, t, 0)),
        out_shape=jax.ShapeDtypeStruct((batch, seq, A_Q_W), _BF),
        scratch_shapes=[pltpu.VMEM((tq + 2 * A_WINDOW, KV_W), _BF)],
        compiler_params=pltpu.CompilerParams(
            dimension_semantics=("arbitrary",) * 2, vmem_limit_bytes=VMEM_LIMIT),
        name="attn_window",
    )(qkv, qkv, qkv, _bias_variants(A_WINDOW, 2), _one_hot_rows(n_pairs * BQ), sink)


def _dilated_kernel(c0, p0, n0, c1, p1, n1, c2, p2, n2, bias_ref, onehot_ref, o_ref,
                    ext0, ext1, lse_ref, acc_ref, *, seq_len):
    t = pl.program_id(1)
    d1, d2 = B_GROUPS[1][1], B_GROUPS[2][1]
    rows0, rows1 = DIL_TILE, DIL_TILE // d1
    n_pairs = B_OUT_W // LANES
    masks = _lane_rows(_BF)
    k_w = n_pairs * LANES

    def spread(src):
        parts = [src[:, 0:k_w]]
        for pair in range(n_pairs):
            v = src[:, k_w + pair * LANES:k_w + (pair + 1) * LANES]
            parts += [v * masks[_LO], v * masks[_HI]]
        return jnp.concatenate(parts, axis=1)

    ext0[0:B_HALO, :] = spread(p0[...])
    ext0[B_HALO:B_HALO + rows0, :] = spread(c0[:, 0:KV_W])
    ext0[B_HALO + rows0:, :] = spread(n0[...])
    for r in range(d1):
        ext1[r, 0:B_HALO, :] = spread(p1[r])
        ext1[r, B_HALO:B_HALO + rows1, :] = spread(c1[r, :, 0:KV_W])
        ext1[r, B_HALO + rows1:, :] = spread(n1[r])

    first_half = lax.broadcasted_iota(jnp.int32, (1, LANES), 1) < HEAD_DIM
    ones_lo = jnp.broadcast_to(masks[_LO], (B_BK, LANES))
    ones_hi = jnp.broadcast_to(masks[_HI], (B_BK, LANES))
    onehot = onehot_ref[...]

    def bias_for(pos0, limit):
        return _pick_bias(bias_ref, pos0 < 0, pos0 + B_BK > limit)

    def unit(q2, kw, v_lo, v_hi, bias):
        s = lax.dot_general(jnp.concatenate([q2, onehot], axis=1),
                            jnp.concatenate([kw, bias], axis=1), _NT,
                            preferred_element_type=_F32)
        m = jnp.max(s, axis=-1, keepdims=True)
        p = jnp.exp2(s - m).astype(_BF)
        p_cat = jnp.concatenate([p[0:BQ], p[BQ:2 * BQ]], axis=1)
        v_cat = jnp.concatenate([jnp.concatenate([v_lo, ones_lo], axis=1),
                                 jnp.concatenate([v_hi, ones_hi], axis=1)], axis=0)
        res = jnp.dot(p_cat, v_cat, preferred_element_type=_F32)
        return jnp.where(first_half, m[0:BQ], m[BQ:2 * BQ]), res[:, LANES:], res[:, 0:LANES]

    def stacked_q(block):
        return jnp.concatenate([block[:, 0:LANES], block[:, LANES:2 * LANES]], axis=0)

    def merge(pair, rows, m_blk, l_blk, pv, last):
        lse_old = lse_ref[pair, rows, :]
        m_new = jnp.maximum(lse_old, m_blk)
        a = jnp.exp2(lse_old - m_new)
        b = jnp.exp2(m_blk - m_new)
        l_new = a + b * l_blk
        acc_ref[pair, rows, :] = (a * acc_ref[pair, rows, :] + b * pv) / l_new
        if not last:
            lse_ref[pair, rows, :] = m_new + jnp.log2(l_new)

    def ext_unit(q_block, ext_window, pair, bias):
        kw = ext_window[:, pair * LANES:(pair + 1) * LANES]
        v0 = k_w + 2 * pair * LANES
        return unit(stacked_q(q_block), kw, ext_window[:, v0:v0 + LANES],
                    ext_window[:, v0 + LANES:v0 + 2 * LANES], bias)

    def group0(i, carry):
        row0 = pl.multiple_of(i * BQ, BQ)
        bias = bias_for(t * DIL_TILE + row0 - B_HALO, seq_len)
        for pair in range(n_pairs):
            q0 = KV_W + 2 * pair * LANES
            m_blk, l_blk, pv = ext_unit(c0[pl.ds(row0, BQ), q0:q0 + 2 * LANES],
                                        ext0[pl.ds(row0, B_BK), :], pair, bias)
            rows = pl.ds(row0, BQ)
            lse_ref[pair, rows, :] = m_blk + jnp.log2(l_blk)
            acc_ref[pair, rows, :] = pv / l_blk
        return carry

    def group1(idx, carry):
        blocks = rows1 // BQ
        r = idx // blocks
        i = idx % blocks
        row0 = pl.multiple_of(i * BQ, BQ)
        bias = bias_for(t * rows1 + row0 - B_HALO, seq_len // d1)
        for pair in range(n_pairs):
            q0 = KV_W + 2 * pair * LANES
            m_blk, l_blk, pv = ext_unit(c1[r, pl.ds(row0, BQ), q0:q0 + 2 * LANES],
                                        ext1[r, pl.ds(row0, B_BK), :], pair, bias)
            merge(pair, pl.ds(row0 * d1 + r, BQ, stride=d1), m_blk, l_blk, pv, last=False)
        return carry

    bias2 = bias_for(t * BQ - B_HALO, seq_len // d2)

    def group2(r, carry):
        for pair in range(n_pairs):
            q0 = KV_W + 2 * pair * LANES
            k_l = pair * LANES
            v_l = k_w + pair * LANES
            kw = jnp.concatenate([p2[r, :, k_l:k_l + LANES], c2[r, :, k_l:k_l + LANES],
                                  n2[r, :, k_l:k_l + LANES]], axis=0)
            vw = jnp.concatenate([p2[r, :, v_l:v_l + LANES], c2[r, :, v_l:v_l + LANES],
                                  n2[r, :, v_l:v_l + LANES]], axis=0)
            m_blk, l_blk, pv = unit(stacked_q(c2[r, :, q0:q0 + 2 * LANES]), kw,
                                    vw * masks[_LO], vw * masks[_HI], bias2)
            merge(pair, pl.ds(r, BQ, stride=d2), m_blk, l_blk, pv, last=True)
        return carry

    lax.fori_loop(0, rows0 // BQ, group0, 0, unroll=DIL_UNROLL)
    lax.fori_loop(0, DIL_TILE // BQ, group1, 0, unroll=DIL_UNROLL)
    lax.fori_loop(0, d2, group2, 0, unroll=DIL_UNROLL)

    def finish(i, carry):
        rows = pl.ds(pl.multiple_of(i * BQ, BQ), BQ)
        for pair in range(n_pairs):
            o_ref[rows, pair * LANES:(pair + 1) * LANES] = acc_ref[pair, rows, :].astype(_BF)
        return carry

    lax.fori_loop(0, DIL_TILE // BQ, finish, 0)


def _dilated_call(g0, g1, g2, batch, seq):
    n_t = seq // DIL_TILE
    d1, d2 = B_GROUPS[1][1], B_GROUPS[2][1]
    rows1, rows2 = DIL_TILE // d1, DIL_TILE // d2
    hb0 = DIL_TILE // B_HALO
    last0 = seq // B_HALO - 1
    prev_t = lambda t: jnp.maximum(t - 1, 0)
    next_t = lambda t: jnp.minimum(t + 1, n_t - 1)
    in_specs = [
        pl.BlockSpec((None, DIL_TILE, SLAB), lambda b, t: (b, t, 0)),
        pl.BlockSpec((None, B_HALO, KV_W), lambda b, t: (b, jnp.maximum(t * hb0 - 1, 0), 0)),
        pl.BlockSpec((None, B_HALO, KV_W), lambda b, t: (b, jnp.minimum((t + 1) * hb0, last0), 0)),
        pl.BlockSpec((None, None, d1, rows1, SLAB), lambda b, t: (b, t, 0, 0, 0)),
        pl.BlockSpec((None, None, d1, B_HALO, KV_W),
                     lambda b, t: (b, prev_t(t), 0, rows1 // B_HALO - 1, 0)),
        pl.BlockSpec((None, None, d1, B_HALO, KV_W), lambda b, t: (b, next_t(t), 0, 0, 0)),
        pl.BlockSpec((None, None, d2, rows2, SLAB), lambda b, t: (b, t, 0, 0, 0)),
        pl.BlockSpec((None, None, d2, B_HALO, KV_W),
                     lambda b, t: (b, prev_t(t), 0, rows2 // B_HALO - 1, 0)),
        pl.BlockSpec((None, None, d2, B_HALO, KV_W), lambda b, t: (b, next_t(t), 0, 0, 0)),
        pl.BlockSpec((4, B_BK, BQ), lambda b, t: (0, 0, 0)),
        pl.BlockSpec((2 * BQ, BQ), lambda b, t: (0, 0)),
    ]
    n_pairs = B_OUT_W // LANES
    ext_w = n_pairs * 3 * LANES
    state = pltpu.VMEM((n_pairs, DIL_TILE, LANES), _F32)
    return pl.pallas_call(
        functools.partial(_dilated_kernel, seq_len=seq),
        grid=(batch, n_t),
        in_specs=in_specs,
        out_specs=pl.BlockSpec((None, DIL_TILE, B_OUT_W), lambda b, t: (b, t, 0)),
        out_shape=jax.ShapeDtypeStruct((batch, seq, B_OUT_W), _BF),
        scratch_shapes=[
            pltpu.VMEM((DIL_TILE + 2 * B_HALO, ext_w), _BF),
            pltpu.VMEM((d1, rows1 + 2 * B_HALO, ext_w), _BF),
            state, state,
        ],
        compiler_params=pltpu.CompilerParams(
            dimension_semantics=("arbitrary",) * 2, vmem_limit_bytes=VMEM_LIMIT),
        name="attn_dilated",
    )(g0, g0, g0, g1, g1, g1, g2, g2, g2, _bias_variants(B_HALO, 1), _one_hot_rows(2 * BQ))


def _out_kernel(x_ref, oa_ref, ob_ref, n1_ref, wga_ref, wgb_ref, wa_ref, wb_ref, wo_ref,
                n2_ref, wg_ref, wu_ref, wd_ref, y_ref, mix_ref, act_ref):
    x = x_ref[...]
    h1 = _rms_rows(x, n1_ref[...]).astype(_BF)
    oa = oa_ref[...]
    ob = ob_ref[...]
    for c in range(D_MODEL // MXU_N):
        cs = slice(c * MXU_N, (c + 1) * MXU_N)
        ga = jax.nn.sigmoid(jnp.dot(h1, wga_ref[:, cs], preferred_element_type=_F32))
        gb = jax.nn.sigmoid(jnp.dot(h1, wgb_ref[:, cs], preferred_element_type=_F32))
        ya = jnp.dot(oa, wa_ref[:, cs], preferred_element_type=_F32)
        yb = jnp.dot(ob, wb_ref[:, cs], preferred_element_type=_F32)
        mix_ref[:, cs] = (ga * ya + gb * yb).astype(_BF)
    x1 = x + jnp.dot(mix_ref[...], wo_ref[...], preferred_element_type=_F32)
    h2 = _rms_rows(x1, n2_ref[...]).astype(_BF)
    for c in range(D_FF // MXU_N):
        cs = slice(c * MXU_N, (c + 1) * MXU_N)
        gate = jnp.dot(h2, wg_ref[:, cs], preferred_element_type=_F32)
        up = jnp.dot(h2, wu_ref[:, cs], preferred_element_type=_F32)
        act_ref[:, cs] = (jax.nn.silu(gate) * up).astype(_BF)
    y_ref[...] = x1 + jnp.dot(act_ref[...], wd_ref[...], preferred_element_type=_F32)


def _out_call(x2d, oa, ob, n1, wga, wgb, wa, wb, wo, n2, wg, wu, wd):
    tokens = x2d.shape[0]
    tm = TOKEN_TILE
    row = lambda i: (i, 0)
    const = lambda i: (0, 0)
    once = dict(pipeline_mode=pl.Buffered(1))
    return pl.pallas_call(
        _out_kernel,
        grid=(tokens // tm,),
        in_specs=[
            pl.BlockSpec((tm, D_MODEL), row),
            pl.BlockSpec((tm, A_Q_W), row),
            pl.BlockSpec((tm, B_OUT_W), row),
            pl.BlockSpec((1, D_MODEL), const),
            pl.BlockSpec((D_MODEL, D_MODEL), const, **once),
            pl.BlockSpec((D_MODEL, D_MODEL), const, **once),
            pl.BlockSpec((A_Q_W, D_MODEL), const, **once),
            pl.BlockSpec((B_OUT_W, D_MODEL), const, **once),
            pl.BlockSpec((D_MODEL, D_MODEL), const, **once),
            pl.BlockSpec((1, D_MODEL), const),
            pl.BlockSpec((D_MODEL, D_FF), const, **once),
            pl.BlockSpec((D_MODEL, D_FF), const, **once),
            pl.BlockSpec((D_FF, D_MODEL), const, **once),
        ],
        out_specs=pl.BlockSpec((tm, D_MODEL), row),
        out_shape=jax.ShapeDtypeStruct((tokens, D_MODEL), _F32),
        scratch_shapes=[pltpu.VMEM((tm, D_MODEL), _BF), pltpu.VMEM((tm, D_FF), _BF)],
        compiler_params=pltpu.CompilerParams(
            dimension_semantics=("arbitrary",), vmem_limit_bytes=VMEM_LIMIT),
        name="out_ffn",
    )(x2d, oa, ob, n1, wga, wgb, wa, wb, wo, n2, wg, wu, wd)


def _prepare_weights(norm1, w_in, qn_a, kn_a, sink_a, qn_b, kn_b, w_br_a, w_br_b, w_out,
                     norm2, w_gate, w_up, w_down):
    w_in0 = w_in[0]
    gate0 = QKV_W
    dims = _pair_lane_dims()
    gains = jnp.stack([qn_a[0][dims] * Q_SCALE, kn_a[0][dims],
                       qn_b[0][dims] * Q_SCALE, kn_b[0][dims]]).astype(_F32)
    sink = sink_a[0].astype(_F32) * math.log2(math.e)
    half_heads = A_Q_HEADS // 2
    sink_pairs = jnp.stack([jnp.repeat(sink[:half_heads], HEAD_DIM).reshape(half_heads, HEAD_DIM),
                            jnp.repeat(sink[half_heads:], HEAD_DIM).reshape(half_heads, HEAD_DIM)],
                           axis=1).reshape(1, A_Q_W)
    return dict(
        n1=norm1[0].reshape(1, D_MODEL).astype(_F32),
        w_qkv=w_in0[:, _qkv_columns()].astype(_BF),
        gains=gains,
        sink=sink_pairs,
        wga=w_in0[:, gate0:gate0 + D_MODEL].astype(_BF),
        wgb=w_in0[:, gate0 + D_MODEL:gate0 + 2 * D_MODEL].astype(_BF),
        wa=w_br_a[0][_oa_rows(), :].astype(_BF),
        wb=w_br_b[0].astype(_BF),
        wo=w_out[0].astype(_BF),
        n2=norm2[0].reshape(1, D_MODEL).astype(_F32),
        wg=w_gate[0].astype(_BF),
        wu=w_up[0].astype(_BF),
        wd=w_down[0].astype(_BF),
    )


def _trunk(x, wts, mean_mat, cos, sin):
    batch, seq, _ = x.shape
    assert seq % DIL_TILE == 0 and seq >= 2 * BQ and seq <= cos.shape[0]
    tokens = batch * seq
    n_t = seq // DIL_TILE
    x2d = x.reshape(tokens, D_MODEL)
    qkv_a, g0, g1, g2 = _qkv_call(x2d, seq, wts["n1"], wts["w_qkv"], wts["gains"], cos, sin,
                                  mean_mat)
    oa = _window_call(qkv_a.reshape(batch, seq, SLAB), wts["sink"])
    ob = _dilated_call(g0.reshape(batch, seq, SLAB),
                       g1.reshape((batch, n_t) + g1.shape[1:]),
                       g2.reshape((batch, n_t) + g2.shape[1:]), batch, seq)
    y = _out_call(x2d, oa.reshape(tokens, A_Q_W), ob.reshape(tokens, B_OUT_W),
                  wts["n1"], wts["wga"], wts["wgb"], wts["wa"], wts["wb"], wts["wo"],
                  wts["n2"], wts["wg"], wts["wu"], wts["wd"])
    return y.reshape(batch, seq, D_MODEL)


def kernel(x_prompt, x_sample, norm1, w_in, qn_a, kn_a, sink_a, qn_b, kn_b, w_br_a, w_br_b,
           w_out, norm2, w_gate, w_up, w_down):
    wts = _prepare_weights(norm1, w_in, qn_a, kn_a, sink_a, qn_b, kn_b, w_br_a, w_br_b, w_out,
                           norm2, w_gate, w_up, w_down)
    mean_mat = _head_mean_matrix()
    cos, sin = _rope_tables(max(x_prompt.shape[1], x_sample.shape[1]))
    return (_trunk(x_prompt, wts, mean_mat, cos, sin), _trunk(x_sample, wts, mean_mat, cos, sin))
```

```python
import functools
import math

import numpy as np
import jax
import jax.numpy as jnp
from jax import lax
from jax.experimental import pallas as pl
from jax.experimental.pallas import tpu as pltpu

D_MODEL = 1024
HEAD_DIM = 64
HALF = HEAD_DIM // 2
A_Q_HEADS = 8
A_KV_HEADS = 2
A_WINDOW = 128
B_GROUPS = ((128, 1), (512, 4), (2048, 16))
B_HEADS_PER_GROUP = 4
B_HEADS = B_HEADS_PER_GROUP * len(B_GROUPS)
D_FF = 2816
ROPE_THETA = 10000.0
EPS = 1e-6
NEG_INF = -1e30

A_Q_W = A_Q_HEADS * HEAD_DIM
A_KV_W = A_KV_HEADS * HEAD_DIM
B_W = B_HEADS * HEAD_DIM
B_OUT_W = B_HEADS_PER_GROUP * HEAD_DIM
PROJ_SLAB = A_Q_W + 2 * A_KV_W
QKV_W = PROJ_SLAB * (1 + len(B_GROUPS))
assert PROJ_SLAB == 3 * B_OUT_W

LANES = 128
MXU_N = 256
VMEM_LIMIT = 56 * 1024 * 1024
BQ = 128
B_HALO = 64
B_BK = BQ + 2 * B_HALO
A_BK = BQ + 2 * A_WINDOW
DIL_TILE = 2048
TOKEN_TILE = 512
QKV_TILE = 512
A_TILE = 1024
assert all(w // (2 * d) == B_HALO for w, d in B_GROUPS)
assert DIL_TILE // B_GROUPS[-1][1] == BQ
Q_SCALE = HEAD_DIM ** -0.5 * math.log2(math.e)

KV_W = 4 * LANES
SLAB = 8 * LANES
_NONE, _H0, _H1, _LO, _HI = range(5)
_A_EMIT = ([[(4 + j, _NONE)] for j in range(4)]
           + [[(0, _H0), (1, _H1)], [(2, _LO), (3, _HI)]])
_B_EMIT = ([[(4 + 2 * j, _H0), (5 + 2 * j, _H1)] for j in range(2)]
           + [[(j, _NONE)] for j in range(2)] + [[(2 + j, _NONE)] for j in range(2)])

_BF = jnp.bfloat16
_F32 = jnp.float32
_NT = (((1,), (1,)), ((), ()))


def _pair_lane_dims():
    lane = np.arange(LANES)
    return (lane // HEAD_DIM) * HALF + lane % HALF


def _pair_cols(h0, h1, base):
    lane = np.arange(LANES)
    head = np.where((lane // HALF) % 2 == 0, h0, h1)
    return base + head * HEAD_DIM + _pair_lane_dims()


def _qkv_columns():
    cols = []
    q_base, k_base, v_base = 0, A_Q_W, A_Q_W + A_KV_W
    for j in range(A_Q_HEADS // 2):
        cols.append(_pair_cols(j, j + A_Q_HEADS // 2, q_base))
    cols.append(_pair_cols(0, 1, k_base))
    cols.append(np.arange(v_base, v_base + A_KV_W))
    qb, kb, vb = PROJ_SLAB, PROJ_SLAB + B_W, PROJ_SLAB + 2 * B_W
    for g in range(len(B_GROUPS)):
        h = g * B_HEADS_PER_GROUP
        cols += [_pair_cols(h, h + 1, qb), _pair_cols(h + 2, h + 3, qb)]
        cols += [_pair_cols(h, h + 1, kb), _pair_cols(h + 2, h + 3, kb)]
        cols.append(np.arange(vb + h * HEAD_DIM, vb + (h + 4) * HEAD_DIM))
    return np.concatenate(cols)


_Q_A, _K_A, _Q_B, _K_B, _V = range(5)
_BLOCK_KINDS = ([_Q_A] * 4 + [_K_A, _V]) + ([_Q_B] * 2 + [_K_B] * 2 + [_V] * 2) * len(B_GROUPS)
_PROJ_BLOCKS = PROJ_SLAB // LANES


def _oa_rows():
    rows = []
    for j in range(A_Q_HEADS // 2):
        rows.append(np.arange(j * HEAD_DIM, (j + 1) * HEAD_DIM))
        rows.append(np.arange((j + 4) * HEAD_DIM, (j + 5) * HEAD_DIM))
    return np.concatenate(rows)


def _rope_tables(seq_len):
    lane = np.arange(LANES)
    inv_freq = ROPE_THETA ** (-jnp.arange(0, HEAD_DIM, 2, dtype=_F32) / HEAD_DIM)
    ang = jnp.arange(seq_len, dtype=_F32)[:, None] * inv_freq[lane % HALF][None, :]
    sign = jnp.where(lane < HEAD_DIM, -1.0, 1.0).astype(_F32)
    return jnp.cos(ang), jnp.sin(ang) * sign[None, :]


def _head_mean_matrix():
    lane = np.arange(MXU_N)
    head = lane // HALF % 2 + 2 * (lane // LANES)
    same = (head[:, None] == head[None, :]).astype(np.float32) / HEAD_DIM
    return jnp.asarray(same, dtype=_BF)


def _one_hot_rows(n_rows):
    return jnp.asarray(np.arange(n_rows)[:, None] % BQ == np.arange(BQ)[None, :], dtype=_BF)


def _bias_variants(halo, copies):
    bk = BQ + 2 * halo
    c = np.arange(bk)[:, None]
    a = np.arange(BQ)[None, :]
    band = (c >= a) & (c <= a + 2 * halo)
    out = []
    for first, last in ((0, 0), (1, 0), (0, 1), (1, 1)):
        keep = band & ((c >= halo) | (not first)) & ((c < bk - halo) | (not last))
        out.append(np.tile(np.where(keep, 0.0, NEG_INF).astype(np.float32), (copies, 1)))
    return jnp.asarray(np.stack(out), dtype=_BF)


def _lane_rows(dtype):
    lane = lax.broadcasted_iota(jnp.int32, (1, LANES), 1)
    head1 = (lane // HALF) % 2 == 1
    hi = lane >= HEAD_DIM
    as_row = lambda m: m.astype(_F32).astype(dtype)
    return [None, as_row(~head1), as_row(head1), as_row(~hi), as_row(hi)]


def _rms_rows(x, gain):
    ms = jnp.mean(x * x, axis=-1, keepdims=True)
    return x * lax.rsqrt(ms + EPS) * gain


def _qkv_kernel(x_ref, n1_ref, w_ref, gains_ref, cos_ref, sin_ref, mean_ref,
                oa_ref, o0_ref, o1_ref, o2_ref, stage_ref):
    h = _rms_rows(x_ref[...], n1_ref[...]).astype(_BF)
    cos = cos_ref[...]
    sin = sin_ref[...]
    masks = _lane_rows(_F32)
    blocks_per_chunk = MXU_N // LANES
    n_chunks = QKV_W // MXU_N
    tm = x_ref.shape[0]

    def project(c):
        return jnp.dot(h, w_ref[:, c * MXU_N:(c + 1) * MXU_N], preferred_element_type=_F32)

    def masked(p, mask):
        return (p if mask == _NONE else p * masks[mask]).astype(_BF)

    def unfold(g, out):
        d = B_GROUPS[g][1]
        for j in range(_PROJ_BLOCKS):
            for r in range(d):
                rows = stage_ref[(g - 1) * _PROJ_BLOCKS + j, pl.ds(r, tm // d, stride=d), :]
                for dest, mask in _B_EMIT[j]:
                    out[r, :, dest * LANES:(dest + 1) * LANES] = masked(rows, mask)

    chunks_per_slab = PROJ_SLAB // MXU_N
    order = [s * chunks_per_slab + k for s in (3, 2, 0, 1) for k in range(chunks_per_slab)]
    p2_next = project(order[0])
    for n, c in enumerate(order):
        kinds = _BLOCK_KINDS[c * blocks_per_chunk:(c + 1) * blocks_per_chunk]
        p2 = p2_next
        if n + 1 < n_chunks:
            p2_next = project(order[n + 1])
        if n == chunks_per_slab:
            unfold(2, o2_ref)
        if n == 2 * chunks_per_slab:
            unfold(1, o1_ref)
        if any(k != _V for k in kinds):
            ms2 = jnp.dot((p2 * p2).astype(_BF), mean_ref[...], preferred_element_type=_F32)
            inv2 = lax.rsqrt(ms2 + EPS)
        for half, kind in enumerate(kinds):
            blk = c * blocks_per_chunk + half
            p = p2[:, half * LANES:(half + 1) * LANES]
            if kind != _V:
                t = p * inv2[:, half * LANES:(half + 1) * LANES] * gains_ref[kind:kind + 1, :]
                p = t * cos + pltpu.roll(t, HEAD_DIM, axis=1) * sin
            slab, j = divmod(blk, _PROJ_BLOCKS)
            if slab < 2:
                out, emit = ((oa_ref, _A_EMIT), (o0_ref, _B_EMIT))[slab]
                for dest, mask in emit[j]:
                    out[:, dest * LANES:(dest + 1) * LANES] = masked(p, mask)
            else:
                stage_ref[(slab - 2) * _PROJ_BLOCKS + j] = p


def _qkv_call(x2d, seq_len, n1, w_qkv, gains, cos, sin, mean_mat):
    tokens = x2d.shape[0]
    tm = QKV_TILE
    tiles_per_seq = seq_len // tm
    sub = DIL_TILE // tm
    n_dil = tokens // DIL_TILE
    d1, d2 = B_GROUPS[1][1], B_GROUPS[2][1]
    const = lambda i: (0, 0)
    tok_sds = jax.ShapeDtypeStruct((tokens, SLAB), _BF)
    return pl.pallas_call(
        _qkv_kernel,
        grid=(tokens // tm,),
        in_specs=[
            pl.BlockSpec((tm, D_MODEL), lambda i: (i, 0)),
            pl.BlockSpec((1, D_MODEL), const),
            pl.BlockSpec((D_MODEL, QKV_W), const, pipeline_mode=pl.Buffered(1)),
            pl.BlockSpec((4, LANES), const),
            pl.BlockSpec((tm, LANES), lambda i: (i % tiles_per_seq, 0)),
            pl.BlockSpec((tm, LANES), lambda i: (i % tiles_per_seq, 0)),
            pl.BlockSpec((MXU_N, MXU_N), const),
        ],
        out_specs=[
            pl.BlockSpec((tm, SLAB), lambda i: (i, 0)),
            pl.BlockSpec((tm, SLAB), lambda i: (i, 0)),
            pl.BlockSpec((None, d1, tm // d1, SLAB), lambda i: (i // sub, 0, i % sub, 0)),
            pl.BlockSpec((None, d2, tm // d2, SLAB), lambda i: (i // sub, 0, i % sub, 0)),
        ],
        out_shape=[
            tok_sds, tok_sds,
            jax.ShapeDtypeStruct((n_dil, d1, DIL_TILE // d1, SLAB), _BF),
            jax.ShapeDtypeStruct((n_dil, d2, DIL_TILE // d2, SLAB), _BF),
        ],
        scratch_shapes=[pltpu.VMEM((2 * _PROJ_BLOCKS, tm, LANES), _F32)],
        compiler_params=pltpu.CompilerParams(
            dimension_semantics=("arbitrary",), vmem_limit_bytes=VMEM_LIMIT),
        name="qkv_proj",
    )(x2d, n1, w_qkv, gains, cos, sin, mean_mat)


def _pick_bias(bias_ref, first, last):
    if first is None and last is None:
        return bias_ref[0]
    if last is None:
        return jnp.where(first, bias_ref[1], bias_ref[0])
    if first is None:
        return jnp.where(last, bias_ref[2], bias_ref[0])
    return jnp.where(first, jnp.where(last, bias_ref[3], bias_ref[1]),
                     jnp.where(last, bias_ref[2], bias_ref[0]))


def _window_kernel(cur_ref, prev_ref, next_ref, bias_ref, onehot_ref, sink_ref, o_ref, kv_ext,
                   *, seq_len):
    tq = cur_ref.shape[0]
    t = pl.program_id(1)
    kv_ext[0:A_WINDOW, :] = prev_ref[...]
    kv_ext[A_WINDOW:A_WINDOW + tq, :] = cur_ref[:, 0:KV_W]
    kv_ext[A_WINDOW + tq:, :] = next_ref[...]

    masks = _lane_rows(_BF)
    first_half = lax.broadcasted_iota(jnp.int32, (1, LANES), 1) < HEAD_DIM
    ones_lo = jnp.broadcast_to(masks[_LO], (A_BK, LANES))
    ones_hi = jnp.broadcast_to(masks[_HI], (A_BK, LANES))
    onehot = onehot_ref[...]
    n_pairs = A_Q_W // LANES

    n_blocks = tq // BQ
    for i in range(n_blocks):
        row0 = i * BQ
        start = t * tq + row0
        bias = _pick_bias(bias_ref,
                          start - A_WINDOW < 0 if i == 0 else None,
                          start + BQ >= seq_len if i == n_blocks - 1 else None)
        win = pl.ds(row0, A_BK)
        k_cat = jnp.concatenate([kv_ext[win, 0:LANES], kv_ext[win, LANES:2 * LANES]], axis=0)
        k_ext = jnp.concatenate([k_cat, bias], axis=1)
        v_cat = jnp.concatenate(
            [jnp.concatenate([kv_ext[win, 2 * LANES:3 * LANES], ones_lo], axis=1),
             jnp.concatenate([kv_ext[win, 3 * LANES:4 * LANES], ones_hi], axis=1)], axis=0)
        q_cat = jnp.concatenate(
            [cur_ref[pl.ds(row0, BQ), KV_W + g * LANES:KV_W + (g + 1) * LANES]
             for g in range(n_pairs)], axis=0)
        q_ext = jnp.concatenate([q_cat, onehot], axis=1)
        s2 = lax.dot_general(q_ext, k_ext, _NT, preferred_element_type=_F32)
        ps, ms = [], []
        for hh in range(2):
            s = s2[:, hh * A_BK:(hh + 1) * A_BK]
            m = jnp.max(s, axis=-1, keepdims=True)
            ms.append(m)
            ps.append(jnp.exp2(s - m).astype(_BF))
        res = jnp.dot(jnp.concatenate(ps, axis=1), v_cat, preferred_element_type=_F32)
        m_blk = jnp.where(first_half, ms[0], ms[1])
        for g in range(n_pairs):
            rows = slice(g * BQ, (g + 1) * BQ)
            sink = sink_ref[:, g * LANES:(g + 1) * LANES]
            m_new = jnp.maximum(sink, m_blk[rows])
            a = jnp.exp2(sink - m_new)
            b = jnp.exp2(m_blk[rows] - m_new)
            out = b * res[rows, 0:LANES] / (a + b * res[rows, LANES:2 * LANES])
            o_ref[pl.ds(row0, BQ), g * LANES:(g + 1) * LANES] = out.astype(_BF)


def _window_call(qkv, sink):
    batch, seq, _ = qkv.shape
    tq = A_TILE
    hb = tq // A_WINDOW
    last_halo = seq // A_WINDOW - 1
    n_pairs = A_Q_W // LANES
    const2 = lambda b, t: (0, 0)
    return pl.pallas_call(
        functools.partial(_window_kernel, seq_len=seq),
        grid=(batch, seq // tq),
        in_specs=[
            pl.BlockSpec((None, tq, SLAB), lambda b, t: (b, t, 0)),
            pl.BlockSpec((None, A_WINDOW, KV_W), lambda b, t: (b, jnp.maximum(t * hb - 1, 0), 0)),
            pl.BlockSpec((None, A_WINDOW, KV_W),
                         lambda b, t: (b, jnp.minimum((t + 1) * hb, last_halo), 0)),
            pl.BlockSpec((4, 2 * A_BK, BQ), lambda b, t: (0, 0, 0)),
            pl.BlockSpec((n_pairs * BQ, BQ), const2),
            pl.BlockSpec((1, A_Q_W), const2),
        ],
        out_specs=pl.BlockSpec((None, tq, A_Q_W), lambda b, t: (b, t, 0)),
        out_shape=jax.ShapeDtypeStruct((batch, seq, A_Q_W), _BF),
        scratch_shapes=[pltpu.VMEM((tq + 2 * A_WINDOW, KV_W), _BF)],
        compiler_params=pltpu.CompilerParams(
            dimension_semantics=("arbitrary",) * 2, vmem_limit_bytes=VMEM_LIMIT),
        name="attn_window",
    )(qkv, qkv, qkv, _bias_variants(A_WINDOW, 2), _one_hot_rows(n_pairs * BQ), sink)


def _dilated_kernel(c0, p0, n0, c1, p1, n1, c2, p2, n2, bias_ref, onehot_ref, o_ref,
                    ext0, ext1, lse_ref, acc_ref, *, seq_len):
    t = pl.program_id(1)
    d1, d2 = B_GROUPS[1][1], B_GROUPS[2][1]
    rows0, rows1 = DIL_TILE, DIL_TILE // d1
    n_pairs = B_OUT_W // LANES
    masks = _lane_rows(_BF)
    k_w = n_pairs * LANES

    def spread(src):
        parts = [src[:, 0:k_w]]
        for pair in range(n_pairs):
            v = src[:, k_w + pair * LANES:k_w + (pair + 1) * LANES]
            parts += [v * masks[_LO], v * masks[_HI]]
        return jnp.concatenate(parts, axis=1)

    ext0[0:B_HALO, :] = spread(p0[...])
    ext0[B_HALO:B_HALO + rows0, :] = spread(c0[:, 0:KV_W])
    ext0[B_HALO + rows0:, :] = spread(n0[...])
    for r in range(d1):
        ext1[r, 0:B_HALO, :] = spread(p1[r])
        ext1[r, B_HALO:B_HALO + rows1, :] = spread(c1[r, :, 0:KV_W])
        ext1[r, B_HALO + rows1:, :] = spread(n1[r])

    first_half = lax.broadcasted_iota(jnp.int32, (1, LANES), 1) < HEAD_DIM
    ones_lo = jnp.broadcast_to(masks[_LO], (B_BK, LANES))
    ones_hi = jnp.broadcast_to(masks[_HI], (B_BK, LANES))
    onehot = onehot_ref[...]

    def bias_for(pos0, limit, i, n_blocks):
        return _pick_bias(bias_ref, pos0 < 0 if i == 0 else None,
                          pos0 + B_BK > limit if i == n_blocks - 1 else None)

    def unit(q2, kw, v_lo, v_hi, bias):
        s = lax.dot_general(jnp.concatenate([q2, onehot], axis=1),
                            jnp.concatenate([kw, bias], axis=1), _NT,
                            preferred_element_type=_F32)
        m = jnp.max(s, axis=-1, keepdims=True)
        p = jnp.exp2(s - m).astype(_BF)
        p_cat = jnp.concatenate([p[0:BQ], p[BQ:2 * BQ]], axis=1)
        v_cat = jnp.concatenate([jnp.concatenate([v_lo, ones_lo], axis=1),
                                 jnp.concatenate([v_hi, ones_hi], axis=1)], axis=0)
        res = jnp.dot(p_cat, v_cat, preferred_element_type=_F32)
        return jnp.where(first_half, m[0:BQ], m[BQ:2 * BQ]), res[:, LANES:], res[:, 0:LANES]

    def stacked_q(block):
        return jnp.concatenate([block[:, 0:LANES], block[:, LANES:2 * LANES]], axis=0)

    def merge(pair, rows, m_blk, l_blk, pv, last):
        lse_old = lse_ref[pair, rows, :]
        m_new = jnp.maximum(lse_old, m_blk)
        a = jnp.exp2(lse_old - m_new)
        b = jnp.exp2(m_blk - m_new)
        l_new = a + b * l_blk
        acc_ref[pair, rows, :] = (a * acc_ref[pair, rows, :] + b * pv) / l_new
        if not last:
            lse_ref[pair, rows, :] = m_new + jnp.log2(l_new)

    def ext_unit(q_block, ext_window, pair, bias):
        kw = ext_window[:, pair * LANES:(pair + 1) * LANES]
        v0 = k_w + 2 * pair * LANES
        return unit(stacked_q(q_block), kw, ext_window[:, v0:v0 + LANES],
                    ext_window[:, v0 + LANES:v0 + 2 * LANES], bias)

    def group0(i):
        row0 = i * BQ
        bias = bias_for(t * DIL_TILE + row0 - B_HALO, seq_len, i, rows0 // BQ)
        for pair in range(n_pairs):
            q0 = KV_W + 2 * pair * LANES
            m_blk, l_blk, pv = ext_unit(c0[pl.ds(row0, BQ), q0:q0 + 2 * LANES],
                                        ext0[pl.ds(row0, B_BK), :], pair, bias)
            rows = pl.ds(row0, BQ)
            lse_ref[pair, rows, :] = m_blk + jnp.log2(l_blk)
            acc_ref[pair, rows, :] = pv / l_blk

    def group1(r, i):
        row0 = i * BQ
        bias = bias_for(t * rows1 + row0 - B_HALO, seq_len // d1, i, rows1 // BQ)
        for pair in range(n_pairs):
            q0 = KV_W + 2 * pair * LANES
            m_blk, l_blk, pv = ext_unit(c1[r, pl.ds(row0, BQ), q0:q0 + 2 * LANES],
                                        ext1[r, pl.ds(row0, B_BK), :], pair, bias)
            merge(pair, pl.ds(row0 * d1 + r, BQ, stride=d1), m_blk, l_blk, pv, last=False)

    bias2 = bias_for(t * BQ - B_HALO, seq_len // d2, 0, 1)

    def group2(r):
        for pair in range(n_pairs):
            q0 = KV_W + 2 * pair * LANES
            k_l = pair * LANES
            v_l = k_w + pair * LANES
            kw = jnp.concatenate([p2[r, :, k_l:k_l + LANES], c2[r, :, k_l:k_l + LANES],
                                  n2[r, :, k_l:k_l + LANES]], axis=0)
            vw = jnp.concatenate([p2[r, :, v_l:v_l + LANES], c2[r, :, v_l:v_l + LANES],
                                  n2[r, :, v_l:v_l + LANES]], axis=0)
            m_blk, l_blk, pv = unit(stacked_q(c2[r, :, q0:q0 + 2 * LANES]), kw,
                                    vw * masks[_LO], vw * masks[_HI], bias2)
            merge(pair, pl.ds(r, BQ, stride=d2), m_blk, l_blk, pv, last=True)

    for i in range(rows0 // BQ):
        group0(i)
    for r in range(d1):
        for i in range(rows1 // BQ):
            group1(r, i)
    for r in range(d2):
        group2(r)

    def finish(i, carry):
        rows = pl.ds(pl.multiple_of(i * BQ, BQ), BQ)
        for pair in range(n_pairs):
            o_ref[rows, pair * LANES:(pair + 1) * LANES] = acc_ref[pair, rows, :].astype(_BF)
        return carry

    lax.fori_loop(0, DIL_TILE // BQ, finish, 0)


def _dilated_call(g0, g1, g2, batch, seq):
    n_t = seq // DIL_TILE
    d1, d2 = B_GROUPS[1][1], B_GROUPS[2][1]
    rows1, rows2 = DIL_TILE // d1, DIL_TILE // d2
    hb0 = DIL_TILE // B_HALO
    last0 = seq // B_HALO - 1
    prev_t = lambda t: jnp.maximum(t - 1, 0)
    next_t = lambda t: jnp.minimum(t + 1, n_t - 1)
    in_specs = [
        pl.BlockSpec((None, DIL_TILE, SLAB), lambda b, t: (b, t, 0)),
        pl.BlockSpec((None, B_HALO, KV_W), lambda b, t: (b, jnp.maximum(t * hb0 - 1, 0), 0)),
        pl.BlockSpec((None, B_HALO, KV_W), lambda b, t: (b, jnp.minimum((t + 1) * hb0, last0), 0)),
        pl.BlockSpec((None, None, d1, rows1, SLAB), lambda b, t: (b, t, 0, 0, 0)),
        pl.BlockSpec((None, None, d1, B_HALO, KV_W),
                     lambda b, t: (b, prev_t(t), 0, rows1 // B_HALO - 1, 0)),
        pl.BlockSpec((None, None, d1, B_HALO, KV_W), lambda b, t: (b, next_t(t), 0, 0, 0)),
        pl.BlockSpec((None, None, d2, rows2, SLAB), lambda b, t: (b, t, 0, 0, 0)),
        pl.BlockSpec((None, None, d2, B_HALO, KV_W),
                     lambda b, t: (b, prev_t(t), 0, rows2 // B_HALO - 1, 0)),
        pl.BlockSpec((None, None, d2, B_HALO, KV_W), lambda b, t: (b, next_t(t), 0, 0, 0)),
        pl.BlockSpec((4, B_BK, BQ), lambda b, t: (0, 0, 0)),
        pl.BlockSpec((2 * BQ, BQ), lambda b, t: (0, 0)),
    ]
    n_pairs = B_OUT_W // LANES
    ext_w = n_pairs * 3 * LANES
    state = pltpu.VMEM((n_pairs, DIL_TILE, LANES), _F32)
    return pl.pallas_call(
        functools.partial(_dilated_kernel, seq_len=seq),
        grid=(batch, n_t),
        in_specs=in_specs,
        out_specs=pl.BlockSpec((None, DIL_TILE, B_OUT_W), lambda b, t: (b, t, 0)),
        out_shape=jax.ShapeDtypeStruct((batch, seq, B_OUT_W), _BF),
        scratch_shapes=[
            pltpu.VMEM((DIL_TILE + 2 * B_HALO, ext_w), _BF),
            pltpu.VMEM((d1, rows1 + 2 * B_HALO, ext_w), _BF),
            state, state,
        ],
        compiler_params=pltpu.CompilerParams(
            dimension_semantics=("arbitrary",) * 2, vmem_limit_bytes=VMEM_LIMIT),
        name="attn_dilated",
    )(g0, g0, g0, g1, g1, g1, g2, g2, g2, _bias_variants(B_HALO, 1), _one_hot_rows(2 * BQ))


def _out_kernel(x_ref, oa_ref, ob_ref, n1_ref, wga_ref, wgb_ref, wa_ref, wb_ref, wo_ref,
                n2_ref, wg_ref, wu_ref, wd_ref, y_ref, mix_ref, act_ref):
    tm = x_ref.shape[0]
    halves = [slice(0, tm // 2), slice(tm // 2, tm)]

    def mix_phase(rows):
        x = x_ref[rows, :]
        h1 = _rms_rows(x, n1_ref[...]).astype(_BF)
        oa = oa_ref[rows, :]
        ob = ob_ref[rows, :]
        for c in range(D_MODEL // MXU_N):
            cs = slice(c * MXU_N, (c + 1) * MXU_N)
            ga = jax.nn.sigmoid(jnp.dot(h1, wga_ref[:, cs], preferred_element_type=_F32))
            gb = jax.nn.sigmoid(jnp.dot(h1, wgb_ref[:, cs], preferred_element_type=_F32))
            ya = jnp.dot(oa, wa_ref[:, cs], preferred_element_type=_F32)
            yb = jnp.dot(ob, wb_ref[:, cs], preferred_element_type=_F32)
            mix_ref[rows, cs] = (ga * ya + gb * yb).astype(_BF)
        return x + jnp.dot(mix_ref[rows, :], wo_ref[...], preferred_element_type=_F32)

    def ffn_phase(rows, x1):
        h2 = _rms_rows(x1, n2_ref[...]).astype(_BF)
        for c in range(D_FF // MXU_N):
            cs = slice(c * MXU_N, (c + 1) * MXU_N)
            gate = jnp.dot(h2, wg_ref[:, cs], preferred_element_type=_F32)
            up = jnp.dot(h2, wu_ref[:, cs], preferred_element_type=_F32)
            act_ref[rows, cs] = (jax.nn.silu(gate) * up).astype(_BF)
        y_ref[rows, :] = x1 + jnp.dot(act_ref[rows, :], wd_ref[...], preferred_element_type=_F32)

    x1s = [mix_phase(rows) for rows in halves]
    for rows, x1 in zip(halves, x1s):
        ffn_phase(rows, x1)


def _out_call(x2d, oa, ob, n1, wga, wgb, wa, wb, wo, n2, wg, wu, wd):
    tokens = x2d.shape[0]
    tm = TOKEN_TILE
    row = lambda i: (i, 0)
    const = lambda i: (0, 0)
    once = dict(pipeline_mode=pl.Buffered(1))
    return pl.pallas_call(
        _out_kernel,
        grid=(tokens // tm,),
        in_specs=[
            pl.BlockSpec((tm, D_MODEL), row),
            pl.BlockSpec((tm, A_Q_W), row),
            pl.BlockSpec((tm, B_OUT_W), row),
            pl.BlockSpec((1, D_MODEL), const),
            pl.BlockSpec((D_MODEL, D_MODEL), const, **once),
            pl.BlockSpec((D_MODEL, D_MODEL), const, **once),
            pl.BlockSpec((A_Q_W, D_MODEL), const, **once),
            pl.BlockSpec((B_OUT_W, D_MODEL), const, **once),
            pl.BlockSpec((D_MODEL, D_MODEL), const, **once),
            pl.BlockSpec((1, D_MODEL), const),
            pl.BlockSpec((D_MODEL, D_FF), const, **once),
            pl.BlockSpec((D_MODEL, D_FF), const, **once),
            pl.BlockSpec((D_FF, D_MODEL), const, **once),
        ],
        out_specs=pl.BlockSpec((tm, D_MODEL), row),
        out_shape=jax.ShapeDtypeStruct((tokens, D_MODEL), _F32),
        scratch_shapes=[pltpu.VMEM((tm, D_MODEL), _BF), pltpu.VMEM((tm, D_FF), _BF)],
        compiler_params=pltpu.CompilerParams(
            dimension_semantics=("arbitrary",), vmem_limit_bytes=VMEM_LIMIT),
        name="out_ffn",
    )(x2d, oa, ob, n1, wga, wgb, wa, wb, wo, n2, wg, wu, wd)


def _prepare_weights(norm1, w_in, qn_a, kn_a, sink_a, qn_b, kn_b, w_br_a, w_br_b, w_out,
                     norm2, w_gate, w_up, w_down):
    w_in0 = w_in[0]
    gate0 = QKV_W
    dims = _pair_lane_dims()
    gains = jnp.stack([qn_a[0][dims] * Q_SCALE, kn_a[0][dims],
                       qn_b[0][dims] * Q_SCALE, kn_b[0][dims]]).astype(_F32)
    sink = sink_a[0].astype(_F32) * math.log2(math.e)
    half_heads = A_Q_HEADS // 2
    sink_pairs = jnp.stack([jnp.repeat(sink[:half_heads], HEAD_DIM).reshape(half_heads, HEAD_DIM),
                            jnp.repeat(sink[half_heads:], HEAD_DIM).reshape(half_heads, HEAD_DIM)],
                           axis=1).reshape(1, A_Q_W)
    return dict(
        n1=norm1[0].reshape(1, D_MODEL).astype(_F32),
        w_qkv=w_in0[:, _qkv_columns()].astype(_BF),
        gains=gains,
        sink=sink_pairs,
        wga=w_in0[:, gate0:gate0 + D_MODEL].astype(_BF),
        wgb=w_in0[:, gate0 + D_MODEL:gate0 + 2 * D_MODEL].astype(_BF),
        wa=w_br_a[0][_oa_rows(), :].astype(_BF),
        wb=w_br_b[0].astype(_BF),
        wo=w_out[0].astype(_BF),
        n2=norm2[0].reshape(1, D_MODEL).astype(_F32),
        wg=w_gate[0].astype(_BF),
        wu=w_up[0].astype(_BF),
        wd=w_down[0].astype(_BF),
    )


def _trunk(x, wts, mean_mat, cos, sin):
    batch, seq, _ = x.shape
    assert seq % DIL_TILE == 0 and seq >= 2 * BQ and seq <= cos.shape[0]
    tokens = batch * seq
    n_t = seq // DIL_TILE
    x2d = x.reshape(tokens, D_MODEL)
    qkv_a, g0, g1, g2 = _qkv_call(x2d, seq, wts["n1"], wts["w_qkv"], wts["gains"], cos, sin,
                                  mean_mat)
    oa = _window_call(qkv_a.reshape(batch, seq, SLAB), wts["sink"])
    ob = _dilated_call(g0.reshape(batch, seq, SLAB),
                       g1.reshape((batch, n_t) + g1.shape[1:]),
                       g2.reshape((batch, n_t) + g2.shape[1:]), batch, seq)
    y = _out_call(x2d, oa.reshape(tokens, A_Q_W), ob.reshape(tokens, B_OUT_W),
                  wts["n1"], wts["wga"], wts["wgb"], wts["wa"], wts["wb"], wts["wo"],
                  wts["n2"], wts["wg"], wts["wu"], wts["wd"])
    return y.reshape(batch, seq, D_MODEL)


def kernel(x_prompt, x_sample, norm1, w_in, qn_a, kn_a, sink_a, qn_b, kn_b, w_br_a, w_br_b,
           w_out, norm2, w_gate, w_up, w_down):
    wts = _prepare_weights(norm1, w_in, qn_a, kn_a, sink_a, qn_b, kn_b, w_br_a, w_br_b, w_out,
                           norm2, w_gate, w_up, w_down)
    mean_mat = _head_mean_matrix()
    cos, sin = _rope_tables(max(x_prompt.shape[1], x_sample.shape[1]))
    return (_trunk(x_prompt, wts, mean_mat, cos, sin), _trunk(x_sample, wts, mean_mat, cos, sin))
```

```python
import functools
import math

import numpy as np
import jax
import jax.numpy as jnp
from jax import lax
from jax.experimental import pallas as pl
from jax.experimental.pallas import tpu as pltpu

D_MODEL = 1024
HEAD_DIM = 64
HALF = HEAD_DIM // 2
A_Q_HEADS = 8
A_KV_HEADS = 2
A_WINDOW = 128
B_GROUPS = ((128, 1), (512, 4), (2048, 16))
B_HEADS_PER_GROUP = 4
B_HEADS = B_HEADS_PER_GROUP * len(B_GROUPS)
D_FF = 2816
ROPE_THETA = 10000.0
EPS = 1e-6
NEG_INF = -1e30

A_Q_W = A_Q_HEADS * HEAD_DIM
A_KV_W = A_KV_HEADS * HEAD_DIM
B_W = B_HEADS * HEAD_DIM
B_OUT_W = B_HEADS_PER_GROUP * HEAD_DIM
PROJ_SLAB = A_Q_W + 2 * A_KV_W
QKV_W = PROJ_SLAB * (1 + len(B_GROUPS))
assert PROJ_SLAB == 3 * B_OUT_W

LANES = 128
SUBLANES = 8
MXU_N = 256
VMEM_LIMIT = 56 * 1024 * 1024
BQ = 128
B_HALO = 64
B_BK = BQ + 2 * B_HALO
A_BK = BQ + 2 * A_WINDOW
DIL_TILE = 2048
TOKEN_TILE = 512
QKV_TILE = 512
A_TILE = 1024
assert all(w // (2 * d) == B_HALO for w, d in B_GROUPS)
assert DIL_TILE // B_GROUPS[-1][1] == BQ
Q_SCALE = HEAD_DIM ** -0.5 * math.log2(math.e)

KV_W = 4 * LANES
SLAB = 8 * LANES
_NONE, _H0, _H1, _LO, _HI = range(5)
_A_EMIT = ([[(4 + j, _NONE)] for j in range(4)]
           + [[(0, _H0), (1, _H1)], [(2, _LO), (3, _HI)]])
_B_EMIT = ([[(4 + 2 * j, _H0), (5 + 2 * j, _H1)] for j in range(2)]
           + [[(j, _NONE)] for j in range(2)] + [[(2 + j, _NONE)] for j in range(2)])

_BF = jnp.bfloat16
_F32 = jnp.float32
_NT = (((1,), (1,)), ((), ()))


def _pair_lane_dims():
    lane = np.arange(LANES)
    return (lane // HEAD_DIM) * HALF + lane % HALF


def _pair_cols(h0, h1, base):
    lane = np.arange(LANES)
    head = np.where((lane // HALF) % 2 == 0, h0, h1)
    return base + head * HEAD_DIM + _pair_lane_dims()


def _qkv_columns():
    cols = []
    q_base, k_base, v_base = 0, A_Q_W, A_Q_W + A_KV_W
    for j in range(A_Q_HEADS // 2):
        cols.append(_pair_cols(j, j + A_Q_HEADS // 2, q_base))
    cols.append(_pair_cols(0, 1, k_base))
    cols.append(np.arange(v_base, v_base + A_KV_W))
    qb, kb, vb = PROJ_SLAB, PROJ_SLAB + B_W, PROJ_SLAB + 2 * B_W
    for g in range(len(B_GROUPS)):
        h = g * B_HEADS_PER_GROUP
        cols += [_pair_cols(h, h + 1, qb), _pair_cols(h + 2, h + 3, qb)]
        cols += [_pair_cols(h, h + 1, kb), _pair_cols(h + 2, h + 3, kb)]
        cols.append(np.arange(vb + h * HEAD_DIM, vb + (h + 4) * HEAD_DIM))
    return np.concatenate(cols)


_Q_A, _K_A, _Q_B, _K_B, _V = range(5)
_BLOCK_KINDS = ([_Q_A] * 4 + [_K_A, _V]) + ([_Q_B] * 2 + [_K_B] * 2 + [_V] * 2) * len(B_GROUPS)
_PROJ_BLOCKS = PROJ_SLAB // LANES


def _oa_rows():
    rows = []
    for j in range(A_Q_HEADS // 2):
        rows.append(np.arange(j * HEAD_DIM, (j + 1) * HEAD_DIM))
        rows.append(np.arange((j + 4) * HEAD_DIM, (j + 5) * HEAD_DIM))
    return np.concatenate(rows)


def _rope_tables(seq_len):
    lane = np.arange(LANES)
    inv_freq = ROPE_THETA ** (-jnp.arange(0, HEAD_DIM, 2, dtype=_F32) / HEAD_DIM)
    ang = jnp.arange(seq_len, dtype=_F32)[:, None] * inv_freq[lane % HALF][None, :]
    sign = jnp.where(lane < HEAD_DIM, -1.0, 1.0).astype(_F32)
    return jnp.cos(ang), jnp.sin(ang) * sign[None, :]


def _head_mean_matrix():
    lane = np.arange(MXU_N)
    head = lane // HALF % 2 + 2 * (lane // LANES)
    same = (head[:, None] == head[None, :]).astype(np.float32) / HEAD_DIM
    return jnp.asarray(same, dtype=_BF)


def _one_hot_rows(n_rows):
    return jnp.asarray(np.arange(n_rows)[:, None] % BQ == np.arange(BQ)[None, :], dtype=_BF)


def _bias_variants(halo, copies):
    bk = BQ + 2 * halo
    c = np.arange(bk)[:, None]
    a = np.arange(BQ)[None, :]
    band = (c >= a) & (c <= a + 2 * halo)
    out = []
    for first, last in ((0, 0), (1, 0), (0, 1), (1, 1)):
        keep = band & ((c >= halo) | (not first)) & ((c < bk - halo) | (not last))
        out.append(np.tile(np.where(keep, 0.0, NEG_INF).astype(np.float32), (copies, 1)))
    return jnp.asarray(np.stack(out), dtype=_BF)


def _lane_rows(dtype):
    lane = lax.broadcasted_iota(jnp.int32, (1, LANES), 1)
    head1 = (lane // HALF) % 2 == 1
    hi = lane >= HEAD_DIM
    as_row = lambda m: m.astype(_F32).astype(dtype)
    return [None, as_row(~head1), as_row(head1), as_row(~hi), as_row(hi)]


def _rms_rows(x, gain):
    ms = jnp.mean(x * x, axis=-1, keepdims=True)
    return x * lax.rsqrt(ms + EPS) * gain


def _qkv_kernel(x_ref, n1_ref, w_ref, gains_ref, cos_ref, sin_ref, mean_ref,
                oa_ref, o0_ref, o1_ref, o2_ref, stage_ref, fold_ref):
    h = _rms_rows(x_ref[...], n1_ref[...]).astype(_BF)
    cos = cos_ref[...]
    sin = sin_ref[...]
    masks = _lane_rows(_F32)
    blocks_per_chunk = MXU_N // LANES
    n_chunks = QKV_W // MXU_N
    tm = x_ref.shape[0]

    def project(c):
        return jnp.dot(h, w_ref[:, c * MXU_N:(c + 1) * MXU_N], preferred_element_type=_F32)

    def masked(p, mask):
        return (p if mask == _NONE else p * masks[mask]).astype(_BF)

    d1, d2 = B_GROUPS[1][1], B_GROUPS[2][1]
    pitch = fold_ref.shape[1] // d2

    def stage(slab, j, p):
        if slab == 2:
            stage_ref[j] = p
        else:
            for g8 in range(tm // 8):
                row, r0 = divmod(g8 * 8, d2)
                fold_ref[j, pl.ds(r0 * pitch + row, 8, stride=pitch), :] = p[g8 * 8:(g8 + 1) * 8]

    def unfold(slab, out):
        d = d1 if slab == 2 else d2
        for j in range(_PROJ_BLOCKS):
            for r in range(d):
                if slab == 2:
                    rows = stage_ref[j, pl.ds(r, tm // d, stride=d), :]
                else:
                    rows = fold_ref[j, r * pitch:r * pitch + tm // d, :]
                for dest, mask in _B_EMIT[j]:
                    out[r, :, dest * LANES:(dest + 1) * LANES] = masked(rows, mask)

    chunks_per_slab = PROJ_SLAB // MXU_N
    order = [s * chunks_per_slab + k for s in (3, 2, 0, 1) for k in range(chunks_per_slab)]
    p2_next = project(order[0])
    for n, c in enumerate(order):
        kinds = _BLOCK_KINDS[c * blocks_per_chunk:(c + 1) * blocks_per_chunk]
        p2 = p2_next
        if n + 1 < n_chunks:
            p2_next = project(order[n + 1])
        if n == chunks_per_slab:
            unfold(3, o2_ref)
        if n == 2 * chunks_per_slab:
            unfold(2, o1_ref)
        if any(k != _V for k in kinds):
            ms2 = jnp.dot((p2 * p2).astype(_BF), mean_ref[...], preferred_element_type=_F32)
            inv2 = lax.rsqrt(ms2 + EPS)
        for half, kind in enumerate(kinds):
            blk = c * blocks_per_chunk + half
            p = p2[:, half * LANES:(half + 1) * LANES]
            if kind != _V:
                t = p * inv2[:, half * LANES:(half + 1) * LANES] * gains_ref[kind:kind + 1, :]
                p = t * cos + pltpu.roll(t, HEAD_DIM, axis=1) * sin
            slab, j = divmod(blk, _PROJ_BLOCKS)
            if slab < 2:
                out, emit = ((oa_ref, _A_EMIT), (o0_ref, _B_EMIT))[slab]
                for dest, mask in emit[j]:
                    out[:, dest * LANES:(dest + 1) * LANES] = masked(p, mask)
            else:
                stage(slab, j, p)


def _qkv_call(x2d, seq_len, n1, w_qkv, gains, cos, sin, mean_mat):
    tokens = x2d.shape[0]
    tm = QKV_TILE
    tiles_per_seq = seq_len // tm
    sub = DIL_TILE // tm
    n_dil = tokens // DIL_TILE
    d1, d2 = B_GROUPS[1][1], B_GROUPS[2][1]
    const = lambda i: (0, 0)
    tok_sds = jax.ShapeDtypeStruct((tokens, SLAB), _BF)
    return pl.pallas_call(
        _qkv_kernel,
        grid=(tokens // tm,),
        in_specs=[
            pl.BlockSpec((tm, D_MODEL), lambda i: (i, 0)),
            pl.BlockSpec((1, D_MODEL), const),
            pl.BlockSpec((D_MODEL, QKV_W), const, pipeline_mode=pl.Buffered(1)),
            pl.BlockSpec((4, LANES), const),
            pl.BlockSpec((tm, LANES), lambda i: (i % tiles_per_seq, 0)),
            pl.BlockSpec((tm, LANES), lambda i: (i % tiles_per_seq, 0)),
            pl.BlockSpec((MXU_N, MXU_N), const),
        ],
        out_specs=[
            pl.BlockSpec((tm, SLAB), lambda i: (i, 0)),
            pl.BlockSpec((tm, SLAB), lambda i: (i, 0)),
            pl.BlockSpec((None, d1, tm // d1, SLAB), lambda i: (i // sub, 0, i % sub, 0)),
            pl.BlockSpec((None, d2, tm // d2, SLAB), lambda i: (i // sub, 0, i % sub, 0)),
        ],
        out_shape=[
            tok_sds, tok_sds,
            jax.ShapeDtypeStruct((n_dil, d1, DIL_TILE // d1, SLAB), _BF),
            jax.ShapeDtypeStruct((n_dil, d2, DIL_TILE // d2, SLAB), _BF),
        ],
        scratch_shapes=[pltpu.VMEM((_PROJ_BLOCKS, tm, LANES), _F32),
                        pltpu.VMEM((_PROJ_BLOCKS, d2 * (tm // d2 + SUBLANES), LANES), _F32)],
        compiler_params=pltpu.CompilerParams(
            dimension_semantics=("arbitrary",), vmem_limit_bytes=VMEM_LIMIT),
        name="qkv_proj",
    )(x2d, n1, w_qkv, gains, cos, sin, mean_mat)


def _pick_bias(bias_ref, first, last):
    if first is None and last is None:
        return bias_ref[0]
    if last is None:
        return jnp.where(first, bias_ref[1], bias_ref[0])
    if first is None:
        return jnp.where(last, bias_ref[2], bias_ref[0])
    return jnp.where(first, jnp.where(last, bias_ref[3], bias_ref[1]),
                     jnp.where(last, bias_ref[2], bias_ref[0]))


def _window_kernel(cur_ref, prev_ref, next_ref, bias_ref, onehot_ref, sink_ref, o_ref, kv_ext,
                   *, seq_len):
    tq = cur_ref.shape[0]
    t = pl.program_id(1)
    kv_ext[0:A_WINDOW, :] = prev_ref[...]
    kv_ext[A_WINDOW:A_WINDOW + tq, :] = cur_ref[:, 0:KV_W]
    kv_ext[A_WINDOW + tq:, :] = next_ref[...]

    masks = _lane_rows(_BF)
    first_half = lax.broadcasted_iota(jnp.int32, (1, LANES), 1) < HEAD_DIM
    ones_lo = jnp.broadcast_to(masks[_LO], (A_BK, LANES))
    ones_hi = jnp.broadcast_to(masks[_HI], (A_BK, LANES))
    onehot = onehot_ref[...]
    n_pairs = A_Q_W // LANES

    n_blocks = tq // BQ
    for i in range(n_blocks):
        row0 = i * BQ
        start = t * tq + row0
        bias = _pick_bias(bias_ref,
                          start - A_WINDOW < 0 if i == 0 else None,
                          start + BQ >= seq_len if i == n_blocks - 1 else None)
        win = pl.ds(row0, A_BK)
        k_cat = jnp.concatenate([kv_ext[win, 0:LANES], kv_ext[win, LANES:2 * LANES]], axis=0)
        k_ext = jnp.concatenate([k_cat, bias], axis=1)
        v_cat = jnp.concatenate(
            [jnp.concatenate([kv_ext[win, 2 * LANES:3 * LANES], ones_lo], axis=1),
             jnp.concatenate([kv_ext[win, 3 * LANES:4 * LANES], ones_hi], axis=1)], axis=0)
        q_cat = jnp.concatenate(
            [cur_ref[pl.ds(row0, BQ), KV_W + g * LANES:KV_W + (g + 1) * LANES]
             for g in range(n_pairs)], axis=0)
        q_ext = jnp.concatenate([q_cat, onehot], axis=1)
        s2 = lax.dot_general(q_ext, k_ext, _NT, preferred_element_type=_F32)
        ps, ms = [], []
        for hh in range(2):
            s = s2[:, hh * A_BK:(hh + 1) * A_BK]
            m = jnp.max(s, axis=-1, keepdims=True)
            ms.append(m)
            ps.append(jnp.exp2(s - m).astype(_BF))
        res = jnp.dot(jnp.concatenate(ps, axis=1), v_cat, preferred_element_type=_F32)
        m_blk = jnp.where(first_half, ms[0], ms[1])
        for g in range(n_pairs):
            rows = slice(g * BQ, (g + 1) * BQ)
            sink = sink_ref[:, g * LANES:(g + 1) * LANES]
            m_new = jnp.maximum(sink, m_blk[rows])
            a = jnp.exp2(sink - m_new)
            b = jnp.exp2(m_blk[rows] - m_new)
            out = b * res[rows, 0:LANES] / (a + b * res[rows, LANES:2 * LANES])
            o_ref[pl.ds(row0, BQ), g * LANES:(g + 1) * LANES] = out.astype(_BF)


def _window_call(qkv, sink):
    batch, seq, _ = qkv.shape
    tq = A_TILE
    hb = tq // A_WINDOW
    last_halo = seq // A_WINDOW - 1
    n_pairs = A_Q_W // LANES
    const2 = lambda b, t: (0, 0)
    return pl.pallas_call(
        functools.partial(_window_kernel, seq_len=seq),
        grid=(batch, seq // tq),
        in_specs=[
            pl.BlockSpec((None, tq, SLAB), lambda b, t: (b, t, 0)),
            pl.BlockSpec((None, A_WINDOW, KV_W), lambda b, t: (b, jnp.maximum(t * hb - 1, 0), 0)),
            pl.BlockSpec((None, A_WINDOW, KV_W),
                         lambda b, t: (b, jnp.minimum((t + 1) * hb, last_halo), 0)),
            pl.BlockSpec((4, 2 * A_BK, BQ), lambda b, t: (0, 0, 0)),
            pl.BlockSpec((n_pairs * BQ, BQ), const2),
            pl.BlockSpec((1, A_Q_W), const2),
        ],
        out_specs=pl.BlockSpec((None, tq, A_Q_W), lambda b, t: (b, t, 0)),
        out_shape=jax.ShapeDtypeStruct((batch, seq, A_Q_W), _BF),
        scratch_shapes=[pltpu.VMEM((tq + 2 * A_WINDOW, KV_W), _BF)],
        compiler_params=pltpu.CompilerParams(
            dimension_semantics=("arbitrary",) * 2, vmem_limit_bytes=VMEM_LIMIT),
        name="attn_window",
    )(qkv, qkv, qkv, _bias_variants(A_WINDOW, 2), _one_hot_rows(n_pairs * BQ), sink)


def _dilated_kernel(c0, p0, n0, c1, p1, n1, c2, p2, n2, bias_ref, onehot_ref, o_ref,
                    ext0, ext1, lse_ref, acc_ref, *, seq_len):
    t = pl.program_id(1)
    d1, d2 = B_GROUPS[1][1], B_GROUPS[2][1]
    rows0, rows1 = DIL_TILE, DIL_TILE // d1
    n_pairs = B_OUT_W // LANES
    masks = _lane_rows(_BF)
    k_w = n_pairs * LANES

    def spread(src):
        parts = [src[:, 0:k_w]]
        for pair in range(n_pairs):
            v = src[:, k_w + pair * LANES:k_w + (pair + 1) * LANES]
            parts += [v * masks[_LO], v * masks[_HI]]
        return jnp.concatenate(parts, axis=1)

    ext0[0:B_HALO, :] = spread(p0[...])
    ext0[B_HALO:B_HALO + rows0, :] = spread(c0[:, 0:KV_W])
    ext0[B_HALO + rows0:, :] = spread(n0[...])
    for r in range(d1):
        ext1[r, 0:B_HALO, :] = spread(p1[r])
        ext1[r, B_HALO:B_HALO + rows1, :] = spread(c1[r, :, 0:KV_W])
        ext1[r, B_HALO + rows1:, :] = spread(n1[r])

    first_half = lax.broadcasted_iota(jnp.int32, (1, LANES), 1) < HEAD_DIM
    ones_lo = jnp.broadcast_to(masks[_LO], (B_BK, LANES))
    ones_hi = jnp.broadcast_to(masks[_HI], (B_BK, LANES))
    onehot = onehot_ref[...]

    def bias_for(pos0, limit, i, n_blocks):
        return _pick_bias(bias_ref, pos0 < 0 if i == 0 else None,
                          pos0 + B_BK > limit if i == n_blocks - 1 else None)

    def unit(q2, kw, v_lo, v_hi, bias):
        s = lax.dot_general(jnp.concatenate([q2, onehot], axis=1),
                            jnp.concatenate([kw, bias], axis=1), _NT,
                            preferred_element_type=_F32)
        m = jnp.max(s, axis=-1, keepdims=True)
        p = jnp.exp2(s - m).astype(_BF)
        p_cat = jnp.concatenate([p[0:BQ], p[BQ:2 * BQ]], axis=1)
        v_cat = jnp.concatenate([jnp.concatenate([v_lo, ones_lo], axis=1),
                                 jnp.concatenate([v_hi, ones_hi], axis=1)], axis=0)
        res = jnp.dot(p_cat, v_cat, preferred_element_type=_F32)
        return jnp.where(first_half, m[0:BQ], m[BQ:2 * BQ]), res[:, LANES:], res[:, 0:LANES]

    def stacked_q(block):
        return jnp.concatenate([block[:, 0:LANES], block[:, LANES:2 * LANES]], axis=0)

    def merge(pair, rows, m_blk, l_blk, pv, last):
        lse_old = lse_ref[pair, rows, :]
        m_new = jnp.maximum(lse_old, m_blk)
        a = jnp.exp2(lse_old - m_new)
        b = jnp.exp2(m_blk - m_new)
        l_new = a + b * l_blk
        acc_ref[pair, rows, :] = (a * acc_ref[pair, rows, :] + b * pv) / l_new
        if not last:
            lse_ref[pair, rows, :] = m_new + jnp.log2(l_new)

    def ext_unit(q_block, ext_window, pair, bias):
        kw = ext_window[:, pair * LANES:(pair + 1) * LANES]
        v0 = k_w + 2 * pair * LANES
        return unit(stacked_q(q_block), kw, ext_window[:, v0:v0 + LANES],
                    ext_window[:, v0 + LANES:v0 + 2 * LANES], bias)

    def group0(i):
        row0 = i * BQ
        bias = bias_for(t * DIL_TILE + row0 - B_HALO, seq_len, i, rows0 // BQ)
        for pair in range(n_pairs):
            q0 = KV_W + 2 * pair * LANES
            m_blk, l_blk, pv = ext_unit(c0[pl.ds(row0, BQ), q0:q0 + 2 * LANES],
                                        ext0[pl.ds(row0, B_BK), :], pair, bias)
            merge(pair, pl.ds(row0, BQ), m_blk, l_blk, pv, last=False)

    def group1(r, i):
        row0 = i * BQ
        bias = bias_for(t * rows1 + row0 - B_HALO, seq_len // d1, i, rows1 // BQ)
        for pair in range(n_pairs):
            q0 = KV_W + 2 * pair * LANES
            m_blk, l_blk, pv = ext_unit(c1[r, pl.ds(row0, BQ), q0:q0 + 2 * LANES],
                                        ext1[r, pl.ds(row0, B_BK), :], pair, bias)
            merge(pair, pl.ds(row0 * d1 + r, BQ, stride=d1), m_blk, l_blk, pv, last=True)

    bias2 = bias_for(t * BQ - B_HALO, seq_len // d2, 0, 1)

    def group2(r):
        for pair in range(n_pairs):
            q0 = KV_W + 2 * pair * LANES
            k_l = pair * LANES
            v_l = k_w + pair * LANES
            kw = jnp.concatenate([p2[r, :, k_l:k_l + LANES], c2[r, :, k_l:k_l + LANES],
                                  n2[r, :, k_l:k_l + LANES]], axis=0)
            vw = jnp.concatenate([p2[r, :, v_l:v_l + LANES], c2[r, :, v_l:v_l + LANES],
                                  n2[r, :, v_l:v_l + LANES]], axis=0)
            m_blk, l_blk, pv = unit(stacked_q(c2[r, :, q0:q0 + 2 * LANES]), kw,
                                    vw * masks[_LO], vw * masks[_HI], bias2)
            rows = pl.ds(r, BQ, stride=d2)
            lse_ref[pair, rows, :] = m_blk + jnp.log2(l_blk)
            acc_ref[pair, rows, :] = pv / l_blk

    for r in range(d2):
        group2(r)
    for i in range(rows0 // BQ):
        group0(i)
    for r in range(d1):
        for i in range(rows1 // BQ):
            group1(r, i)

    def finish(i, carry):
        rows = pl.ds(pl.multiple_of(i * BQ, BQ), BQ)
        for pair in range(n_pairs):
            o_ref[rows, pair * LANES:(pair + 1) * LANES] = acc_ref[pair, rows, :].astype(_BF)
        return carry

    lax.fori_loop(0, DIL_TILE // BQ, finish, 0)


def _dilated_call(g0, g1, g2, batch, seq):
    n_t = seq // DIL_TILE
    d1, d2 = B_GROUPS[1][1], B_GROUPS[2][1]
    rows1, rows2 = DIL_TILE // d1, DIL_TILE // d2
    hb0 = DIL_TILE // B_HALO
    last0 = seq // B_HALO - 1
    prev_t = lambda t: jnp.maximum(t - 1, 0)
    next_t = lambda t: jnp.minimum(t + 1, n_t - 1)
    in_specs = [
        pl.BlockSpec((None, DIL_TILE, SLAB), lambda b, t: (b, t, 0)),
        pl.BlockSpec((None, B_HALO, KV_W), lambda b, t: (b, jnp.maximum(t * hb0 - 1, 0), 0)),
        pl.BlockSpec((None, B_HALO, KV_W), lambda b, t: (b, jnp.minimum((t + 1) * hb0, last0), 0)),
        pl.BlockSpec((None, None, d1, rows1, SLAB), lambda b, t: (b, t, 0, 0, 0)),
        pl.BlockSpec((None, None, d1, B_HALO, KV_W),
                     lambda b, t: (b, prev_t(t), 0, rows1 // B_HALO - 1, 0)),
        pl.BlockSpec((None, None, d1, B_HALO, KV_W), lambda b, t: (b, next_t(t), 0, 0, 0)),
        pl.BlockSpec((None, None, d2, rows2, SLAB), lambda b, t: (b, t, 0, 0, 0)),
        pl.BlockSpec((None, None, d2, B_HALO, KV_W),
                     lambda b, t: (b, prev_t(t), 0, rows2 // B_HALO - 1, 0)),
        pl.BlockSpec((None, None, d2, B_HALO, KV_W), lambda b, t: (b, next_t(t), 0, 0, 0)),
        pl.BlockSpec((4, B_BK, BQ), lambda b, t: (0, 0, 0)),
        pl.BlockSpec((2 * BQ, BQ), lambda b, t: (0, 0)),
    ]
    n_pairs = B_OUT_W // LANES
    ext_w = n_pairs * 3 * LANES
    state = pltpu.VMEM((n_pairs, DIL_TILE, LANES), _F32)
    return pl.pallas_call(
        functools.partial(_dilated_kernel, seq_len=seq),
        grid=(batch, n_t),
        in_specs=in_specs,
        out_specs=pl.BlockSpec((None, DIL_TILE, B_OUT_W), lambda b, t: (b, t, 0)),
        out_shape=jax.ShapeDtypeStruct((batch, seq, B_OUT_W), _BF),
        scratch_shapes=[
            pltpu.VMEM((DIL_TILE + 2 * B_HALO, ext_w), _BF),
            pltpu.VMEM((d1, rows1 + 2 * B_HALO, ext_w), _BF),
            state, state,
        ],
        compiler_params=pltpu.CompilerParams(
            dimension_semantics=("arbitrary",) * 2, vmem_limit_bytes=VMEM_LIMIT),
        name="attn_dilated",
    )(g0, g0, g0, g1, g1, g1, g2, g2, g2, _bias_variants(B_HALO, 1), _one_hot_rows(2 * BQ))


def _out_kernel(x_ref, oa_ref, ob_ref, n1_ref, wga_ref, wgb_ref, wa_ref, wb_ref, wo_ref,
                n2_ref, wg_ref, wu_ref, wd_ref, y_ref, mix_ref, act_ref):
    x = x_ref[...]
    h1 = _rms_rows(x, n1_ref[...]).astype(_BF)
    oa = oa_ref[...]
    ob = ob_ref[...]
    for c in range(D_MODEL // MXU_N):
        cs = slice(c * MXU_N, (c + 1) * MXU_N)
        ga = jax.nn.sigmoid(jnp.dot(h1, wga_ref[:, cs], preferred_element_type=_F32))
        gb = jax.nn.sigmoid(jnp.dot(h1, wgb_ref[:, cs], preferred_element_type=_F32))
        ya = jnp.dot(oa, wa_ref[:, cs], preferred_element_type=_F32)
        yb = jnp.dot(ob, wb_ref[:, cs], preferred_element_type=_F32)
        mix_ref[:, cs] = (ga * ya + gb * yb).astype(_BF)
    x1 = x + jnp.dot(mix_ref[...], wo_ref[...], preferred_element_type=_F32)
    h2 = _rms_rows(x1, n2_ref[...]).astype(_BF)
    for c in range(D_FF // MXU_N):
        cs = slice(c * MXU_N, (c + 1) * MXU_N)
        gate = jnp.dot(h2, wg_ref[:, cs], preferred_element_type=_F32)
        up = jnp.dot(h2, wu_ref[:, cs], preferred_element_type=_F32)
        act_ref[:, cs] = (jax.nn.silu(gate) * up).astype(_BF)
    y_ref[...] = x1 + jnp.dot(act_ref[...], wd_ref[...], preferred_element_type=_F32)


def _out_call(x2d, oa, ob, n1, wga, wgb, wa, wb, wo, n2, wg, wu, wd):
    tokens = x2d.shape[0]
    tm = TOKEN_TILE
    row = lambda i: (i, 0)
    const = lambda i: (0, 0)
    once = dict(pipeline_mode=pl.Buffered(1))
    return pl.pallas_call(
        _out_kernel,
        grid=(tokens // tm,),
        in_specs=[
            pl.BlockSpec((tm, D_MODEL), row),
            pl.BlockSpec((tm, A_Q_W), row),
            pl.BlockSpec((tm, B_OUT_W), row),
            pl.BlockSpec((1, D_MODEL), const),
            pl.BlockSpec((D_MODEL, D_MODEL), const, **once),
            pl.BlockSpec((D_MODEL, D_MODEL), const, **once),
            pl.BlockSpec((A_Q_W, D_MODEL), const, **once),
            pl.BlockSpec((B_OUT_W, D_MODEL), const, **once),
            pl.BlockSpec((D_MODEL, D_MODEL), const, **once),
            pl.BlockSpec((1, D_MODEL), const),
            pl.BlockSpec((D_MODEL, D_FF), const, **once),
            pl.BlockSpec((D_MODEL, D_FF), const, **once),
            pl.BlockSpec((D_FF, D_MODEL), const, **once),
        ],
        out_specs=pl.BlockSpec((tm, D_MODEL), row),
        out_shape=jax.ShapeDtypeStruct((tokens, D_MODEL), _F32),
        scratch_shapes=[pltpu.VMEM((tm, D_MODEL), _BF), pltpu.VMEM((tm, D_FF), _BF)],
        compiler_params=pltpu.CompilerParams(
            dimension_semantics=("arbitrary",), vmem_limit_bytes=VMEM_LIMIT),
        name="out_ffn",
    )(x2d, oa, ob, n1, wga, wgb, wa, wb, wo, n2, wg, wu, wd)


def _prepare_weights(norm1, w_in, qn_a, kn_a, sink_a, qn_b, kn_b, w_br_a, w_br_b, w_out,
                     norm2, w_gate, w_up, w_down):
    w_in0 = w_in[0]
    gate0 = QKV_W
    dims = _pair_lane_dims()
    gains = jnp.stack([qn_a[0][dims] * Q_SCALE, kn_a[0][dims],
                       qn_b[0][dims] * Q_SCALE, kn_b[0][dims]]).astype(_F32)
    sink = sink_a[0].astype(_F32) * math.log2(math.e)
    half_heads = A_Q_HEADS // 2
    sink_pairs = jnp.stack([jnp.repeat(sink[:half_heads], HEAD_DIM).reshape(half_heads, HEAD_DIM),
                            jnp.repeat(sink[half_heads:], HEAD_DIM).reshape(half_heads, HEAD_DIM)],
                           axis=1).reshape(1, A_Q_W)
    return dict(
        n1=norm1[0].reshape(1, D_MODEL).astype(_F32),
        w_qkv=w_in0[:, _qkv_columns()].astype(_BF),
        gains=gains,
        sink=sink_pairs,
        wga=w_in0[:, gate0:gate0 + D_MODEL].astype(_BF),
        wgb=w_in0[:, gate0 + D_MODEL:gate0 + 2 * D_MODEL].astype(_BF),
        wa=w_br_a[0][_oa_rows(), :].astype(_BF),
        wb=w_br_b[0].astype(_BF),
        wo=w_out[0].astype(_BF),
        n2=norm2[0].reshape(1, D_MODEL).astype(_F32),
        wg=w_gate[0].astype(_BF),
        wu=w_up[0].astype(_BF),
        wd=w_down[0].astype(_BF),
    )


def _trunk(x, wts, mean_mat, cos, sin):
    batch, seq, _ = x.shape
    assert seq % DIL_TILE == 0 and seq >= 2 * BQ and seq <= cos.shape[0]
    tokens = batch * seq
    n_t = seq // DIL_TILE
    x2d = x.reshape(tokens, D_MODEL)
    qkv_a, g0, g1, g2 = _qkv_call(x2d, seq, wts["n1"], wts["w_qkv"], wts["gains"], cos, sin,
                                  mean_mat)
    oa = _window_call(qkv_a.reshape(batch, seq, SLAB), wts["sink"])
    ob = _dilated_call(g0.reshape(batch, seq, SLAB),
                       g1.reshape((batch, n_t) + g1.shape[1:]),
                       g2.reshape((batch, n_t) + g2.shape[1:]), batch, seq)
    y = _out_call(x2d, oa.reshape(tokens, A_Q_W), ob.reshape(tokens, B_OUT_W),
                  wts["n1"], wts["wga"], wts["wgb"], wts["wa"], wts["wb"], wts["wo"],
                  wts["n2"], wts["wg"], wts["wu"], wts["wd"])
    return y.reshape(batch, seq, D_MODEL)


def kernel(x_prompt, x_sample, norm1, w_in, qn_a, kn_a, sink_a, qn_b, kn_b, w_br_a, w_br_b,
           w_out, norm2, w_gate, w_up, w_down):
    wts = _prepare_weights(norm1, w_in, qn_a, kn_a, sink_a, qn_b, kn_b, w_br_a, w_br_b, w_out,
                           norm2, w_gate, w_up, w_down)
    mean_mat = _head_mean_matrix()
    cos, sin = _rope_tables(max(x_prompt.shape[1], x_sample.shape[1]))
    return (_trunk(x_prompt, wts, mean_mat, cos, sin), _trunk(x_sample, wts, mean_mat, cos, sin))
```

```python
import functools
import math

import numpy as np
import jax
import jax.numpy as jnp
from jax import lax
from jax.experimental import pallas as pl
from jax.experimental.pallas import tpu as pltpu

D_MODEL = 1024
HEAD_DIM = 64
HALF = HEAD_DIM // 2
A_Q_HEADS = 8
A_KV_HEADS = 2
A_WINDOW = 128
B_GROUPS = ((128, 1), (512, 4), (2048, 16))
B_HEADS_PER_GROUP = 4
B_HEADS = B_HEADS_PER_GROUP * len(B_GROUPS)
D_FF = 2816
ROPE_THETA = 10000.0
EPS = 1e-6
NEG_INF = -1e30

A_Q_W = A_Q_HEADS * HEAD_DIM
A_KV_W = A_KV_HEADS * HEAD_DIM
B_W = B_HEADS * HEAD_DIM
B_OUT_W = B_HEADS_PER_GROUP * HEAD_DIM
PROJ_SLAB = A_Q_W + 2 * A_KV_W
QKV_W = PROJ_SLAB * (1 + len(B_GROUPS))
assert PROJ_SLAB == 3 * B_OUT_W

LANES = 128
SUBLANES = 8
MXU_N = 256
VMEM_LIMIT = 56 * 1024 * 1024
BQ = 128
B_HALO = 64
B_BK = BQ + 2 * B_HALO
A_BK = BQ + 2 * A_WINDOW
DIL_TILE = 2048
TOKEN_TILE = 512
QKV_TILE = 1024
QKV_SUBTILES = 2
A_TILE = 2048
assert all(w // (2 * d) == B_HALO for w, d in B_GROUPS)
assert DIL_TILE // B_GROUPS[-1][1] == BQ
Q_SCALE = HEAD_DIM ** -0.5 * math.log2(math.e)

KV_W = 4 * LANES
SLAB = 8 * LANES
_NONE, _H0, _H1, _LO, _HI = range(5)
_A_EMIT = ([[(4 + j, _NONE)] for j in range(4)]
           + [[(0, _H0), (1, _H1)], [(2, _LO), (3, _HI)]])
_B_EMIT = ([[(4 + 2 * j, _H0), (5 + 2 * j, _H1)] for j in range(2)]
           + [[(j, _NONE)] for j in range(2)] + [[(2 + j, _NONE)] for j in range(2)])

_BF = jnp.bfloat16
_F32 = jnp.float32
_NT = (((1,), (1,)), ((), ()))


def _pair_lane_dims():
    lane = np.arange(LANES)
    return (lane // HEAD_DIM) * HALF + lane % HALF


def _pair_cols(h0, h1, base):
    lane = np.arange(LANES)
    head = np.where((lane // HALF) % 2 == 0, h0, h1)
    return base + head * HEAD_DIM + _pair_lane_dims()


def _qkv_columns():
    cols = []
    q_base, k_base, v_base = 0, A_Q_W, A_Q_W + A_KV_W
    for j in range(A_Q_HEADS // 2):
        cols.append(_pair_cols(j, j + A_Q_HEADS // 2, q_base))
    cols.append(_pair_cols(0, 1, k_base))
    cols.append(np.arange(v_base, v_base + A_KV_W))
    qb, kb, vb = PROJ_SLAB, PROJ_SLAB + B_W, PROJ_SLAB + 2 * B_W
    for g in range(len(B_GROUPS)):
        h = g * B_HEADS_PER_GROUP
        cols += [_pair_cols(h, h + 1, qb), _pair_cols(h + 2, h + 3, qb)]
        cols += [_pair_cols(h, h + 1, kb), _pair_cols(h + 2, h + 3, kb)]
        cols.append(np.arange(vb + h * HEAD_DIM, vb + (h + 4) * HEAD_DIM))
    return np.concatenate(cols)


_Q_A, _K_A, _Q_B, _K_B, _V = range(5)
_BLOCK_KINDS = ([_Q_A] * 4 + [_K_A, _V]) + ([_Q_B] * 2 + [_K_B] * 2 + [_V] * 2) * len(B_GROUPS)
_PROJ_BLOCKS = PROJ_SLAB // LANES


def _oa_rows():
    rows = []
    for j in range(A_Q_HEADS // 2):
        rows.append(np.arange(j * HEAD_DIM, (j + 1) * HEAD_DIM))
        rows.append(np.arange((j + 4) * HEAD_DIM, (j + 5) * HEAD_DIM))
    return np.concatenate(rows)


def _rope_tables(seq_len):
    lane = np.arange(LANES)
    inv_freq = ROPE_THETA ** (-jnp.arange(0, HEAD_DIM, 2, dtype=_F32) / HEAD_DIM)
    ang = jnp.arange(seq_len, dtype=_F32)[:, None] * inv_freq[lane % HALF][None, :]
    sign = jnp.where(lane < HEAD_DIM, -1.0, 1.0).astype(_F32)
    return jnp.cos(ang), jnp.sin(ang) * sign[None, :]


def _head_mean_matrix():
    lane = np.arange(MXU_N)
    head = lane // HALF % 2 + 2 * (lane // LANES)
    same = (head[:, None] == head[None, :]).astype(np.float32) / HEAD_DIM
    return jnp.asarray(same, dtype=_BF)


def _one_hot_rows(n_rows):
    return jnp.asarray(np.arange(n_rows)[:, None] % BQ == np.arange(BQ)[None, :], dtype=_BF)


def _bias_variants(halo, copies):
    bk = BQ + 2 * halo
    c = np.arange(bk)[:, None]
    a = np.arange(BQ)[None, :]
    band = (c >= a) & (c <= a + 2 * halo)
    out = []
    for first, last in ((0, 0), (1, 0), (0, 1), (1, 1)):
        keep = band & ((c >= halo) | (not first)) & ((c < bk - halo) | (not last))
        out.append(np.tile(np.where(keep, 0.0, NEG_INF).astype(np.float32), (copies, 1)))
    return jnp.asarray(np.stack(out), dtype=_BF)


def _lane_rows(dtype):
    lane = lax.broadcasted_iota(jnp.int32, (1, LANES), 1)
    head1 = (lane // HALF) % 2 == 1
    hi = lane >= HEAD_DIM
    as_row = lambda m: m.astype(_F32).astype(dtype)
    return [None, as_row(~head1), as_row(head1), as_row(~hi), as_row(hi)]


def _rms_rows(x, gain):
    ms = jnp.mean(x * x, axis=-1, keepdims=True)
    return x * lax.rsqrt(ms + EPS) * gain


def _qkv_kernel(x_ref, n1_ref, w_ref, gains_ref, cos_ref, sin_ref, mean_ref,
                oa_ref, o0_ref, o1_ref, o2_ref, stage_ref, fold_ref):
    tm = x_ref.shape[0] // QKV_SUBTILES
    for s in range(QKV_SUBTILES):
        _qkv_subtile(s, tm, x_ref, n1_ref, w_ref, gains_ref, cos_ref, sin_ref, mean_ref,
                     oa_ref, o0_ref, o1_ref, o2_ref, stage_ref.at[s], fold_ref.at[s])


def _qkv_subtile(s, tm, x_ref, n1_ref, w_ref, gains_ref, cos_ref, sin_ref, mean_ref,
                 oa_ref, o0_ref, o1_ref, o2_ref, stage_ref, fold_ref):
    tok = slice(s * tm, (s + 1) * tm)
    h = _rms_rows(x_ref[tok, :], n1_ref[...]).astype(_BF)
    cos = cos_ref[tok, :]
    sin = sin_ref[tok, :]
    masks = _lane_rows(_F32)
    blocks_per_chunk = MXU_N // LANES
    n_chunks = QKV_W // MXU_N

    def project(c):
        return jnp.dot(h, w_ref[:, c * MXU_N:(c + 1) * MXU_N], preferred_element_type=_F32)

    def masked(p, mask):
        return (p if mask == _NONE else p * masks[mask]).astype(_BF)

    d1, d2 = B_GROUPS[1][1], B_GROUPS[2][1]
    pitch = fold_ref.shape[1] // d2

    def stage(slab, j, p):
        if slab == 2:
            stage_ref[j] = p
        else:
            for g8 in range(tm // 8):
                row, r0 = divmod(g8 * 8, d2)
                fold_ref[j, pl.ds(r0 * pitch + row, 8, stride=pitch), :] = p[g8 * 8:(g8 + 1) * 8]

    def unfold(slab, out):
        d = d1 if slab == 2 else d2
        for j in range(_PROJ_BLOCKS):
            for r in range(d):
                if slab == 2:
                    rows = stage_ref[j, pl.ds(r, tm // d, stride=d), :]
                else:
                    rows = fold_ref[j, r * pitch:r * pitch + tm // d, :]
                for dest, mask in _B_EMIT[j]:
                    out[r, s * (tm // d):(s + 1) * (tm // d),
                        dest * LANES:(dest + 1) * LANES] = masked(rows, mask)

    chunks_per_slab = PROJ_SLAB // MXU_N
    order = [s * chunks_per_slab + k for s in (3, 2, 0, 1) for k in range(chunks_per_slab)]
    p2_next = project(order[0])
    for n, c in enumerate(order):
        kinds = _BLOCK_KINDS[c * blocks_per_chunk:(c + 1) * blocks_per_chunk]
        p2 = p2_next
        if n + 1 < n_chunks:
            p2_next = project(order[n + 1])
        if n == chunks_per_slab:
            unfold(3, o2_ref)
        if n == 2 * chunks_per_slab:
            unfold(2, o1_ref)
        if any(k != _V for k in kinds):
            ms2 = jnp.dot((p2 * p2).astype(_BF), mean_ref[...], preferred_element_type=_F32)
            inv2 = lax.rsqrt(ms2 + EPS)
        for half, kind in enumerate(kinds):
            blk = c * blocks_per_chunk + half
            p = p2[:, half * LANES:(half + 1) * LANES]
            if kind != _V:
                t = p * inv2[:, half * LANES:(half + 1) * LANES] * gains_ref[kind:kind + 1, :]
                p = t * cos + pltpu.roll(t, HEAD_DIM, axis=1) * sin
            slab, j = divmod(blk, _PROJ_BLOCKS)
            if slab < 2:
                out, emit = ((oa_ref, _A_EMIT), (o0_ref, _B_EMIT))[slab]
                for dest, mask in emit[j]:
                    out[tok, dest * LANES:(dest + 1) * LANES] = masked(p, mask)
            else:
                stage(slab, j, p)


def _qkv_call(x2d, seq_len, n1, w_qkv, gains, cos, sin, mean_mat):
    tokens = x2d.shape[0]
    tm = QKV_TILE
    sub_tm = tm // QKV_SUBTILES
    tiles_per_seq = seq_len // tm
    sub = DIL_TILE // tm
    n_dil = tokens // DIL_TILE
    d1, d2 = B_GROUPS[1][1], B_GROUPS[2][1]
    const = lambda i: (0, 0)
    tok_sds = jax.ShapeDtypeStruct((tokens, SLAB), _BF)
    return pl.pallas_call(
        _qkv_kernel,
        grid=(tokens // tm,),
        in_specs=[
            pl.BlockSpec((tm, D_MODEL), lambda i: (i, 0)),
            pl.BlockSpec((1, D_MODEL), const),
            pl.BlockSpec((D_MODEL, QKV_W), const, pipeline_mode=pl.Buffered(1)),
            pl.BlockSpec((4, LANES), const),
            pl.BlockSpec((tm, LANES), lambda i: (i % tiles_per_seq, 0)),
            pl.BlockSpec((tm, LANES), lambda i: (i % tiles_per_seq, 0)),
            pl.BlockSpec((MXU_N, MXU_N), const),
        ],
        out_specs=[
            pl.BlockSpec((tm, SLAB), lambda i: (i, 0)),
            pl.BlockSpec((tm, SLAB), lambda i: (i, 0)),
            pl.BlockSpec((None, d1, tm // d1, SLAB), lambda i: (i // sub, 0, i % sub, 0)),
            pl.BlockSpec((None, d2, tm // d2, SLAB), lambda i: (i // sub, 0, i % sub, 0)),
        ],
        out_shape=[
            tok_sds, tok_sds,
            jax.ShapeDtypeStruct((n_dil, d1, DIL_TILE // d1, SLAB), _BF),
            jax.ShapeDtypeStruct((n_dil, d2, DIL_TILE // d2, SLAB), _BF),
        ],
        scratch_shapes=[
            pltpu.VMEM((QKV_SUBTILES, _PROJ_BLOCKS, sub_tm, LANES), _F32),
            pltpu.VMEM((QKV_SUBTILES, _PROJ_BLOCKS, d2 * (sub_tm // d2 + SUBLANES), LANES), _F32)],
        compiler_params=pltpu.CompilerParams(
            dimension_semantics=("arbitrary",), vmem_limit_bytes=VMEM_LIMIT),
        name="qkv_proj",
    )(x2d, n1, w_qkv, gains, cos, sin, mean_mat)


def _pick_bias(bias_ref, first, last):
    if first is None and last is None:
        return bias_ref[0]
    if last is None:
        return jnp.where(first, bias_ref[1], bias_ref[0])
    if first is None:
        return jnp.where(last, bias_ref[2], bias_ref[0])
    return jnp.where(first, jnp.where(last, bias_ref[3], bias_ref[1]),
                     jnp.where(last, bias_ref[2], bias_ref[0]))


def _window_kernel(cur_ref, prev_ref, next_ref, bias_ref, onehot_ref, sink_ref, o_ref, kv_ext,
                   *, seq_len):
    tq = cur_ref.shape[0]
    t = pl.program_id(1)
    kv_ext[0:A_WINDOW, :] = prev_ref[...]
    kv_ext[A_WINDOW:A_WINDOW + tq, :] = cur_ref[:, 0:KV_W]
    kv_ext[A_WINDOW + tq:, :] = next_ref[...]

    masks = _lane_rows(_BF)
    first_half = lax.broadcasted_iota(jnp.int32, (1, LANES), 1) < HEAD_DIM
    ones_lo = jnp.broadcast_to(masks[_LO], (A_BK, LANES))
    ones_hi = jnp.broadcast_to(masks[_HI], (A_BK, LANES))
    onehot = onehot_ref[...]
    n_pairs = A_Q_W // LANES

    n_blocks = tq // BQ
    for i in range(n_blocks):
        row0 = i * BQ
        start = t * tq + row0
        bias = _pick_bias(bias_ref,
                          start - A_WINDOW < 0 if i == 0 else None,
                          start + BQ >= seq_len if i == n_blocks - 1 else None)
        win = pl.ds(row0, A_BK)
        k_cat = jnp.concatenate([kv_ext[win, 0:LANES], kv_ext[win, LANES:2 * LANES]], axis=0)
        k_ext = jnp.concatenate([k_cat, bias], axis=1)
        v_cat = jnp.concatenate(
            [jnp.concatenate([kv_ext[win, 2 * LANES:3 * LANES], ones_lo], axis=1),
             jnp.concatenate([kv_ext[win, 3 * LANES:4 * LANES], ones_hi], axis=1)], axis=0)
        q_cat = jnp.concatenate(
            [cur_ref[pl.ds(row0, BQ), KV_W + g * LANES:KV_W + (g + 1) * LANES]
             for g in range(n_pairs)], axis=0)
        q_ext = jnp.concatenate([q_cat, onehot], axis=1)
        s2 = lax.dot_general(q_ext, k_ext, _NT, preferred_element_type=_F32)
        ps, ms = [], []
        for hh in range(2):
            s = s2[:, hh * A_BK:(hh + 1) * A_BK]
            m = jnp.max(s, axis=-1, keepdims=True)
            ms.append(m)
            ps.append(jnp.exp2(s - m).astype(_BF))
        res = jnp.dot(jnp.concatenate(ps, axis=1), v_cat, preferred_element_type=_F32)
        m_blk = jnp.where(first_half, ms[0], ms[1])
        for g in range(n_pairs):
            rows = slice(g * BQ, (g + 1) * BQ)
            sink = sink_ref[:, g * LANES:(g + 1) * LANES]
            m_new = jnp.maximum(sink, m_blk[rows])
            a = jnp.exp2(sink - m_new)
            b = jnp.exp2(m_blk[rows] - m_new)
            out = b * res[rows, 0:LANES] / (a + b * res[rows, LANES:2 * LANES])
            o_ref[pl.ds(row0, BQ), g * LANES:(g + 1) * LANES] = out.astype(_BF)


def _window_call(qkv, sink):
    batch, seq, _ = qkv.shape
    tq = A_TILE
    hb = tq // A_WINDOW
    last_halo = seq // A_WINDOW - 1
    n_pairs = A_Q_W // LANES
    const2 = lambda b, t: (0, 0)
    return pl.pallas_call(
        functools.partial(_window_kernel, seq_len=seq),
        grid=(batch, seq // tq),
        in_specs=[
            pl.BlockSpec((None, tq, SLAB), lambda b, t: (b, t, 0)),
            pl.BlockSpec((None, A_WINDOW, KV_W), lambda b, t: (b, jnp.maximum(t * hb - 1, 0), 0)),
            pl.BlockSpec((None, A_WINDOW, KV_W),
                         lambda b, t: (b, jnp.minimum((t + 1) * hb, last_halo), 0)),
            pl.BlockSpec((4, 2 * A_BK, BQ), lambda b, t: (0, 0, 0)),
            pl.BlockSpec((n_pairs * BQ, BQ), const2),
            pl.BlockSpec((1, A_Q_W), const2),
        ],
        out_specs=pl.BlockSpec((None, tq, A_Q_W), lambda b, t: (b, t, 0)),
        out_shape=jax.ShapeDtypeStruct((batch, seq, A_Q_W), _BF),
        scratch_shapes=[pltpu.VMEM((tq + 2 * A_WINDOW, KV_W), _BF)],
        compiler_params=pltpu.CompilerParams(
            dimension_semantics=("arbitrary",) * 2, vmem_limit_bytes=VMEM_LIMIT),
        name="attn_window",
    )(qkv, qkv, qkv, _bias_variants(A_WINDOW, 2), _one_hot_rows(n_pairs * BQ), sink)


def _dilated_kernel(c0, p0, n0, c1, p1, n1, c2, p2, n2, bias_ref, onehot_ref, o_ref,
                    ext0, ext1, lse_ref, acc_ref, *, seq_len):
    t = pl.program_id(1)
    d1, d2 = B_GROUPS[1][1], B_GROUPS[2][1]
    rows0, rows1 = DIL_TILE, DIL_TILE // d1
    n_pairs = B_OUT_W // LANES
    masks = _lane_rows(_BF)
    k_w = n_pairs * LANES

    def spread(src):
        parts = [src[:, 0:k_w]]
        for pair in range(n_pairs):
            v = src[:, k_w + pair * LANES:k_w + (pair + 1) * LANES]
            parts += [v * masks[_LO], v * masks[_HI]]
        return jnp.concatenate(parts, axis=1)

    ext0[0:B_HALO, :] = spread(p0[...])
    ext0[B_HALO:B_HALO + rows0, :] = spread(c0[:, 0:KV_W])
    ext0[B_HALO + rows0:, :] = spread(n0[...])
    for r in range(d1):
        ext1[r, 0:B_HALO, :] = spread(p1[r])
        ext1[r, B_HALO:B_HALO + rows1, :] = spread(c1[r, :, 0:KV_W])
        ext1[r, B_HALO + rows1:, :] = spread(n1[r])

    first_half = lax.broadcasted_iota(jnp.int32, (1, LANES), 1) < HEAD_DIM
    ones_lo = jnp.broadcast_to(masks[_LO], (B_BK, LANES))
    ones_hi = jnp.broadcast_to(masks[_HI], (B_BK, LANES))
    onehot = onehot_ref[...]

    def bias_for(pos0, limit, i, n_blocks):
        return _pick_bias(bias_ref, pos0 < 0 if i == 0 else None,
                          pos0 + B_BK > limit if i == n_blocks - 1 else None)

    def unit(q2, kw, v_lo, v_hi, bias):
        s = lax.dot_general(jnp.concatenate([q2, onehot], axis=1),
                            jnp.concatenate([kw, bias], axis=1), _NT,
                            preferred_element_type=_F32)
        m = jnp.max(s, axis=-1, keepdims=True)
        p = jnp.exp2(s - m).astype(_BF)
        p_cat = jnp.concatenate([p[0:BQ], p[BQ:2 * BQ]], axis=1)
        v_cat = jnp.concatenate([jnp.concatenate([v_lo, ones_lo], axis=1),
                                 jnp.concatenate([v_hi, ones_hi], axis=1)], axis=0)
        res = jnp.dot(p_cat, v_cat, preferred_element_type=_F32)
        return jnp.where(first_half, m[0:BQ], m[BQ:2 * BQ]), res[:, LANES:], res[:, 0:LANES]

    def stacked_q(block):
        return jnp.concatenate([block[:, 0:LANES], block[:, LANES:2 * LANES]], axis=0)

    def merge(pair, rows, m_blk, l_blk, pv, last):
        lse_old = lse_ref[pair, rows, :]
        m_new = jnp.maximum(lse_old, m_blk)
        a = jnp.exp2(lse_old - m_new)
        b = jnp.exp2(m_blk - m_new)
        l_new = a + b * l_blk
        acc_ref[pair, rows, :] = (a * acc_ref[pair, rows, :] + b * pv) / l_new
        if not last:
            lse_ref[pair, rows, :] = m_new + jnp.log2(l_new)

    def ext_unit(q_block, ext_window, pair, bias):
        kw = ext_window[:, pair * LANES:(pair + 1) * LANES]
        v0 = k_w + 2 * pair * LANES
        return unit(stacked_q(q_block), kw, ext_window[:, v0:v0 + LANES],
                    ext_window[:, v0 + LANES:v0 + 2 * LANES], bias)

    def group0(i):
        row0 = i * BQ
        bias = bias_for(t * DIL_TILE + row0 - B_HALO, seq_len, i, rows0 // BQ)
        for pair in range(n_pairs):
            q0 = KV_W + 2 * pair * LANES
            m_blk, l_blk, pv = ext_unit(c0[pl.ds(row0, BQ), q0:q0 + 2 * LANES],
                                        ext0[pl.ds(row0, B_BK), :], pair, bias)
            rows = pl.ds(row0, BQ)
            lse_ref[pair, rows, :] = m_blk + jnp.log2(l_blk)
            acc_ref[pair, rows, :] = pv / l_blk

    def group1(r, i):
        row0 = i * BQ
        bias = bias_for(t * rows1 + row0 - B_HALO, seq_len // d1, i, rows1 // BQ)
        for pair in range(n_pairs):
            q0 = KV_W + 2 * pair * LANES
            m_blk, l_blk, pv = ext_unit(c1[r, pl.ds(row0, BQ), q0:q0 + 2 * LANES],
                                        ext1[r, pl.ds(row0, B_BK), :], pair, bias)
            merge(pair, pl.ds(row0 * d1 + r, BQ, stride=d1), m_blk, l_blk, pv, last=False)

    bias2 = bias_for(t * BQ - B_HALO, seq_len // d2, 0, 1)

    def group2(r):
        for pair in range(n_pairs):
            q0 = KV_W + 2 * pair * LANES
            k_l = pair * LANES
            v_l = k_w + pair * LANES
            kw = jnp.concatenate([p2[r, :, k_l:k_l + LANES], c2[r, :, k_l:k_l + LANES],
                                  n2[r, :, k_l:k_l + LANES]], axis=0)
            vw = jnp.concatenate([p2[r, :, v_l:v_l + LANES], c2[r, :, v_l:v_l + LANES],
                                  n2[r, :, v_l:v_l + LANES]], axis=0)
            m_blk, l_blk, pv = unit(stacked_q(c2[r, :, q0:q0 + 2 * LANES]), kw,
                                    vw * masks[_LO], vw * masks[_HI], bias2)
            merge(pair, pl.ds(r, BQ, stride=d2), m_blk, l_blk, pv, last=True)

    for i in range(rows0 // BQ):
        group0(i)
    for r in range(d1):
        for i in range(rows1 // BQ):
            group1(r, i)
    for r in range(d2):
        group2(r)

    def finish(i, carry):
        rows = pl.ds(pl.multiple_of(i * BQ, BQ), BQ)
        for pair in range(n_pairs):
            o_ref[rows, pair * LANES:(pair + 1) * LANES] = acc_ref[pair, rows, :].astype(_BF)
        return carry

    lax.fori_loop(0, DIL_TILE // BQ, finish, 0)


def _dilated_call(g0, g1, g2, batch, seq):
    n_t = seq // DIL_TILE
    d1, d2 = B_GROUPS[1][1], B_GROUPS[2][1]
    rows1, rows2 = DIL_TILE // d1, DIL_TILE // d2
    hb0 = DIL_TILE // B_HALO
    last0 = seq // B_HALO - 1
    prev_t = lambda t: jnp.maximum(t - 1, 0)
    next_t = lambda t: jnp.minimum(t + 1, n_t - 1)
    in_specs = [
        pl.BlockSpec((None, DIL_TILE, SLAB), lambda b, t: (b, t, 0)),
        pl.BlockSpec((None, B_HALO, KV_W), lambda b, t: (b, jnp.maximum(t * hb0 - 1, 0), 0)),
        pl.BlockSpec((None, B_HALO, KV_W), lambda b, t: (b, jnp.minimum((t + 1) * hb0, last0), 0)),
        pl.BlockSpec((None, None, d1, rows1, SLAB), lambda b, t: (b, t, 0, 0, 0)),
        pl.BlockSpec((None, None, d1, B_HALO, KV_W),
                     lambda b, t: (b, prev_t(t), 0, rows1 // B_HALO - 1, 0)),
        pl.BlockSpec((None, None, d1, B_HALO, KV_W), lambda b, t: (b, next_t(t), 0, 0, 0)),
        pl.BlockSpec((None, None, d2, rows2, SLAB), lambda b, t: (b, t, 0, 0, 0)),
        pl.BlockSpec((None, None, d2, B_HALO, KV_W),
                     lambda b, t: (b, prev_t(t), 0, rows2 // B_HALO - 1, 0)),
        pl.BlockSpec((None, None, d2, B_HALO, KV_W), lambda b, t: (b, next_t(t), 0, 0, 0)),
        pl.BlockSpec((4, B_BK, BQ), lambda b, t: (0, 0, 0)),
        pl.BlockSpec((2 * BQ, BQ), lambda b, t: (0, 0)),
    ]
    n_pairs = B_OUT_W // LANES
    ext_w = n_pairs * 3 * LANES
    state = pltpu.VMEM((n_pairs, DIL_TILE, LANES), _F32)
    return pl.pallas_call(
        functools.partial(_dilated_kernel, seq_len=seq),
        grid=(batch, n_t),
        in_specs=in_specs,
        out_specs=pl.BlockSpec((None, DIL_TILE, B_OUT_W), lambda b, t: (b, t, 0)),
        out_shape=jax.ShapeDtypeStruct((batch, seq, B_OUT_W), _BF),
        scratch_shapes=[
            pltpu.VMEM((DIL_TILE + 2 * B_HALO, ext_w), _BF),
            pltpu.VMEM((d1, rows1 + 2 * B_HALO, ext_w), _BF),
            state, state,
        ],
        compiler_params=pltpu.CompilerParams(
            dimension_semantics=("arbitrary",) * 2, vmem_limit_bytes=VMEM_LIMIT),
        name="attn_dilated",
    )(g0, g0, g0, g1, g1, g1, g2, g2, g2, _bias_variants(B_HALO, 1), _one_hot_rows(2 * BQ))


def _out_kernel(x_ref, oa_ref, ob_ref, n1_ref, wga_ref, wgb_ref, wa_ref, wb_ref, wo_ref,
                n2_ref, wg_ref, wu_ref, wd_ref, y_ref, mix_ref, act_ref):
    x = x_ref[...]
    h1 = _rms_rows(x, n1_ref[...]).astype(_BF)
    oa = oa_ref[...]
    ob = ob_ref[...]
    for c in range(D_MODEL // MXU_N):
        cs = slice(c * MXU_N, (c + 1) * MXU_N)
        ga = jax.nn.sigmoid(jnp.dot(h1, wga_ref[:, cs], preferred_element_type=_F32))
        gb = jax.nn.sigmoid(jnp.dot(h1, wgb_ref[:, cs], preferred_element_type=_F32))
        ya = jnp.dot(oa, wa_ref[:, cs], preferred_element_type=_F32)
        yb = jnp.dot(ob, wb_ref[:, cs], preferred_element_type=_F32)
        mix_ref[:, cs] = (ga * ya + gb * yb).astype(_BF)
    x1 = x + jnp.dot(mix_ref[...], wo_ref[...], preferred_element_type=_F32)
    h2 = _rms_rows(x1, n2_ref[...]).astype(_BF)
    for c in range(D_FF // MXU_N):
        cs = slice(c * MXU_N, (c + 1) * MXU_N)
        gate = jnp.dot(h2, wg_ref[:, cs], preferred_element_type=_F32)
        up = jnp.dot(h2, wu_ref[:, cs], preferred_element_type=_F32)
        act_ref[:, cs] = (jax.nn.silu(gate) * up).astype(_BF)
    y_ref[...] = x1 + jnp.dot(act_ref[...], wd_ref[...], preferred_element_type=_F32)


def _out_call(x2d, oa, ob, n1, wga, wgb, wa, wb, wo, n2, wg, wu, wd):
    tokens = x2d.shape[0]
    tm = TOKEN_TILE
    row = lambda i: (i, 0)
    const = lambda i: (0, 0)
    once = dict(pipeline_mode=pl.Buffered(1))
    return pl.pallas_call(
        _out_kernel,
        grid=(tokens // tm,),
        in_specs=[
            pl.BlockSpec((tm, D_MODEL), row),
            pl.BlockSpec((tm, A_Q_W), row),
            pl.BlockSpec((tm, B_OUT_W), row),
            pl.BlockSpec((1, D_MODEL), const),
            pl.BlockSpec((D_MODEL, D_MODEL), const, **once),
            pl.BlockSpec((D_MODEL, D_MODEL), const, **once),
            pl.BlockSpec((A_Q_W, D_MODEL), const, **once),
            pl.BlockSpec((B_OUT_W, D_MODEL), const, **once),
            pl.BlockSpec((D_MODEL, D_MODEL), const, **once),
            pl.BlockSpec((1, D_MODEL), const),
            pl.BlockSpec((D_MODEL, D_FF), const, **once),
            pl.BlockSpec((D_MODEL, D_FF), const, **once),
            pl.BlockSpec((D_FF, D_MODEL), const, **once),
        ],
        out_specs=pl.BlockSpec((tm, D_MODEL), row),
        out_shape=jax.ShapeDtypeStruct((tokens, D_MODEL), _F32),
        scratch_shapes=[pltpu.VMEM((tm, D_MODEL), _BF), pltpu.VMEM((tm, D_FF), _BF)],
        compiler_params=pltpu.CompilerParams(
            dimension_semantics=("arbitrary",), vmem_limit_bytes=VMEM_LIMIT),
        name="out_ffn",
    )(x2d, oa, ob, n1, wga, wgb, wa, wb, wo, n2, wg, wu, wd)


def _prepare_weights(norm1, w_in, qn_a, kn_a, sink_a, qn_b, kn_b, w_br_a, w_br_b, w_out,
                     norm2, w_gate, w_up, w_down):
    w_in0 = w_in[0]
    gate0 = QKV_W
    dims = _pair_lane_dims()
    gains = jnp.stack([qn_a[0][dims] * Q_SCALE, kn_a[0][dims],
                       qn_b[0][dims] * Q_SCALE, kn_b[0][dims]]).astype(_F32)
    sink = sink_a[0].astype(_F32) * math.log2(math.e)
    half_heads = A_Q_HEADS // 2
    sink_pairs = jnp.stack([jnp.repeat(sink[:half_heads], HEAD_DIM).reshape(half_heads, HEAD_DIM),
                            jnp.repeat(sink[half_heads:], HEAD_DIM).reshape(half_heads, HEAD_DIM)],
                           axis=1).reshape(1, A_Q_W)
    return dict(
        n1=norm1[0].reshape(1, D_MODEL).astype(_F32),
        w_qkv=w_in0[:, _qkv_columns()].astype(_BF),
        gains=gains,
        sink=sink_pairs,
        wga=w_in0[:, gate0:gate0 + D_MODEL].astype(_BF),
        wgb=w_in0[:, gate0 + D_MODEL:gate0 + 2 * D_MODEL].astype(_BF),
        wa=w_br_a[0][_oa_rows(), :].astype(_BF),
        wb=w_br_b[0].astype(_BF),
        wo=w_out[0].astype(_BF),
        n2=norm2[0].reshape(1, D_MODEL).astype(_F32),
        wg=w_gate[0].astype(_BF),
        wu=w_up[0].astype(_BF),
        wd=w_down[0].astype(_BF),
    )


def _trunk(x, wts, mean_mat, cos, sin):
    batch, seq, _ = x.shape
    assert seq % DIL_TILE == 0 and seq >= 2 * BQ and seq <= cos.shape[0]
    tokens = batch * seq
    n_t = seq // DIL_TILE
    x2d = x.reshape(tokens, D_MODEL)
    qkv_a, g0, g1, g2 = _qkv_call(x2d, seq, wts["n1"], wts["w_qkv"], wts["gains"], cos, sin,
                                  mean_mat)
    oa = _window_call(qkv_a.reshape(batch, seq, SLAB), wts["sink"])
    ob = _dilated_call(g0.reshape(batch, seq, SLAB),
                       g1.reshape((batch, n_t) + g1.shape[1:]),
                       g2.reshape((batch, n_t) + g2.shape[1:]), batch, seq)
    y = _out_call(x2d, oa.reshape(tokens, A_Q_W), ob.reshape(tokens, B_OUT_W),
                  wts["n1"], wts["wga"], wts["wgb"], wts["wa"], wts["wb"], wts["wo"],
                  wts["n2"], wts["wg"], wts["wu"], wts["wd"])
    return y.reshape(batch, seq, D_MODEL)


def kernel(x_prompt, x_sample, norm1, w_in, qn_a, kn_a, sink_a, qn_b, kn_b, w_br_a, w_br_b,
           w_out, norm2, w_gate, w_up, w_down):
    wts = _prepare_weights(norm1, w_in, qn_a, kn_a, sink_a, qn_b, kn_b, w_br_a, w_br_b, w_out,
                           norm2, w_gate, w_up, w_down)
    mean_mat = _head_mean_matrix()
    cos, sin = _rope_tables(max(x_prompt.shape[1], x_sample.shape[1]))
    return (_trunk(x_prompt, wts, mean_mat, cos, sin), _trunk(x_sample, wts, mean_mat, cos, sin))
```

```python
import functools
import math

import numpy as np
import jax
import jax.numpy as jnp
from jax import lax
from jax.experimental import pallas as pl
from jax.experimental.pallas import tpu as pltpu

D_MODEL = 1024
HEAD_DIM = 64
HALF = HEAD_DIM // 2
A_Q_HEADS = 8
A_KV_HEADS = 2
A_WINDOW = 128
B_GROUPS = ((128, 1), (512, 4), (2048, 16))
B_HEADS_PER_GROUP = 4
B_HEADS = B_HEADS_PER_GROUP * len(B_GROUPS)
D_FF = 2816
ROPE_THETA = 10000.0
EPS = 1e-6
NEG_INF = -1e30

A_Q_W = A_Q_HEADS * HEAD_DIM
A_KV_W = A_KV_HEADS * HEAD_DIM
B_W = B_HEADS * HEAD_DIM
B_OUT_W = B_HEADS_PER_GROUP * HEAD_DIM
PROJ_SLAB = A_Q_W + 2 * A_KV_W
QKV_W = PROJ_SLAB * (1 + len(B_GROUPS))
assert PROJ_SLAB == 3 * B_OUT_W

LANES = 128
SUBLANES = 8
MXU_N = 256
VMEM_LIMIT = 56 * 1024 * 1024
BQ = 128
B_HALO = 64
B_BK = BQ + 2 * B_HALO
A_BK = BQ + 2 * A_WINDOW
DIL_TILE = 2048
TOKEN_TILE = 512
QKV_TILE = 1024
QKV_SUBTILES = 2
A_TILE = 4096
assert all(w // (2 * d) == B_HALO for w, d in B_GROUPS)
assert DIL_TILE // B_GROUPS[-1][1] == BQ
Q_SCALE = HEAD_DIM ** -0.5 * math.log2(math.e)

KV_W = 4 * LANES
SLAB = 8 * LANES
_NONE, _H0, _H1, _LO, _HI = range(5)
_A_EMIT = ([[(4 + j, _NONE)] for j in range(4)]
           + [[(0, _H0), (1, _H1)], [(2, _LO), (3, _HI)]])
_B_EMIT = ([[(4 + 2 * j, _H0), (5 + 2 * j, _H1)] for j in range(2)]
           + [[(j, _NONE)] for j in range(2)] + [[(2 + j, _NONE)] for j in range(2)])

_BF = jnp.bfloat16
_F32 = jnp.float32
_NT = (((1,), (1,)), ((), ()))


def _pair_lane_dims():
    lane = np.arange(LANES)
    return (lane // HEAD_DIM) * HALF + lane % HALF


def _pair_cols(h0, h1, base):
    lane = np.arange(LANES)
    head = np.where((lane // HALF) % 2 == 0, h0, h1)
    return base + head * HEAD_DIM + _pair_lane_dims()


def _qkv_columns():
    cols = []
    q_base, k_base, v_base = 0, A_Q_W, A_Q_W + A_KV_W
    for j in range(A_Q_HEADS // 2):
        cols.append(_pair_cols(j, j + A_Q_HEADS // 2, q_base))
    cols.append(_pair_cols(0, 1, k_base))
    cols.append(np.arange(v_base, v_base + A_KV_W))
    qb, kb, vb = PROJ_SLAB, PROJ_SLAB + B_W, PROJ_SLAB + 2 * B_W
    for g in range(len(B_GROUPS)):
        h = g * B_HEADS_PER_GROUP
        cols += [_pair_cols(h, h + 1, qb), _pair_cols(h + 2, h + 3, qb)]
        cols += [_pair_cols(h, h + 1, kb), _pair_cols(h + 2, h + 3, kb)]
        cols.append(np.arange(vb + h * HEAD_DIM, vb + (h + 4) * HEAD_DIM))
    return np.concatenate(cols)


_Q_A, _K_A, _Q_B, _K_B, _V = range(5)
_BLOCK_KINDS = ([_Q_A] * 4 + [_K_A, _V]) + ([_Q_B] * 2 + [_K_B] * 2 + [_V] * 2) * len(B_GROUPS)
_PROJ_BLOCKS = PROJ_SLAB // LANES


def _oa_rows():
    rows = []
    for j in range(A_Q_HEADS // 2):
        rows.append(np.arange(j * HEAD_DIM, (j + 1) * HEAD_DIM))
        rows.append(np.arange((j + 4) * HEAD_DIM, (j + 5) * HEAD_DIM))
    return np.concatenate(rows)


def _rope_tables(seq_len):
    lane = np.arange(LANES)
    inv_freq = ROPE_THETA ** (-jnp.arange(0, HEAD_DIM, 2, dtype=_F32) / HEAD_DIM)
    ang = jnp.arange(seq_len, dtype=_F32)[:, None] * inv_freq[lane % HALF][None, :]
    sign = jnp.where(lane < HEAD_DIM, -1.0, 1.0).astype(_F32)
    return jnp.cos(ang), jnp.sin(ang) * sign[None, :]


def _head_mean_matrix():
    lane = np.arange(MXU_N)
    head = lane // HALF % 2 + 2 * (lane // LANES)
    same = (head[:, None] == head[None, :]).astype(np.float32) / HEAD_DIM
    return jnp.asarray(same, dtype=_BF)


def _one_hot_rows(n_rows):
    return jnp.asarray(np.arange(n_rows)[:, None] % BQ == np.arange(BQ)[None, :], dtype=_BF)


def _bias_variants(halo, copies):
    bk = BQ + 2 * halo
    c = np.arange(bk)[:, None]
    a = np.arange(BQ)[None, :]
    band = (c >= a) & (c <= a + 2 * halo)
    out = []
    for first, last in ((0, 0), (1, 0), (0, 1), (1, 1)):
        keep = band & ((c >= halo) | (not first)) & ((c < bk - halo) | (not last))
        out.append(np.tile(np.where(keep, 0.0, NEG_INF).astype(np.float32), (copies, 1)))
    return jnp.asarray(np.stack(out), dtype=_BF)


def _lane_rows(dtype):
    lane = lax.broadcasted_iota(jnp.int32, (1, LANES), 1)
    head1 = (lane // HALF) % 2 == 1
    hi = lane >= HEAD_DIM
    as_row = lambda m: m.astype(_F32).astype(dtype)
    return [None, as_row(~head1), as_row(head1), as_row(~hi), as_row(hi)]


def _rms_rows(x, gain):
    ms = jnp.mean(x * x, axis=-1, keepdims=True)
    return x * lax.rsqrt(ms + EPS) * gain


def _qkv_kernel(x_ref, n1_ref, w_ref, gains_ref, cos_ref, sin_ref, mean_ref,
                oa_ref, o0_ref, o1_ref, o2_ref, stage_ref, fold_ref):
    tm = x_ref.shape[0] // QKV_SUBTILES
    for s in range(QKV_SUBTILES):
        _qkv_subtile(s, tm, x_ref, n1_ref, w_ref, gains_ref, cos_ref, sin_ref, mean_ref,
                     oa_ref, o0_ref, o1_ref, o2_ref, stage_ref.at[s], fold_ref.at[s])


def _qkv_subtile(s, tm, x_ref, n1_ref, w_ref, gains_ref, cos_ref, sin_ref, mean_ref,
                 oa_ref, o0_ref, o1_ref, o2_ref, stage_ref, fold_ref):
    tok = slice(s * tm, (s + 1) * tm)
    h = _rms_rows(x_ref[tok, :], n1_ref[...]).astype(_BF)
    cos = cos_ref[tok, :]
    sin = sin_ref[tok, :]
    masks = _lane_rows(_F32)
    blocks_per_chunk = MXU_N // LANES
    n_chunks = QKV_W // MXU_N

    def project(c):
        return jnp.dot(h, w_ref[:, c * MXU_N:(c + 1) * MXU_N], preferred_element_type=_F32)

    def masked(p, mask):
        return (p if mask == _NONE else p * masks[mask]).astype(_BF)

    d1, d2 = B_GROUPS[1][1], B_GROUPS[2][1]
    pitch = fold_ref.shape[1] // d2

    def stage(slab, j, p):
        if slab == 2:
            stage_ref[j] = p
        else:
            for g8 in range(tm // 8):
                row, r0 = divmod(g8 * 8, d2)
                fold_ref[j, pl.ds(r0 * pitch + row, 8, stride=pitch), :] = p[g8 * 8:(g8 + 1) * 8]

    def unfold(slab, out):
        d = d1 if slab == 2 else d2
        for j in range(_PROJ_BLOCKS):
            for r in range(d):
                if slab == 2:
                    rows = stage_ref[j, pl.ds(r, tm // d, stride=d), :]
                else:
                    rows = fold_ref[j, r * pitch:r * pitch + tm // d, :]
                for dest, mask in _B_EMIT[j]:
                    out[r, s * (tm // d):(s + 1) * (tm // d),
                        dest * LANES:(dest + 1) * LANES] = masked(rows, mask)

    chunks_per_slab = PROJ_SLAB // MXU_N
    order = [s * chunks_per_slab + k for s in (3, 2, 0, 1) for k in range(chunks_per_slab)]
    p2_next = project(order[0])
    for n, c in enumerate(order):
        kinds = _BLOCK_KINDS[c * blocks_per_chunk:(c + 1) * blocks_per_chunk]
        p2 = p2_next
        if n + 1 < n_chunks:
            p2_next = project(order[n + 1])
        if n == chunks_per_slab:
            unfold(3, o2_ref)
        if n == 2 * chunks_per_slab:
            unfold(2, o1_ref)
        if any(k != _V for k in kinds):
            ms2 = jnp.dot((p2 * p2).astype(_BF), mean_ref[...], preferred_element_type=_F32)
            inv2 = lax.rsqrt(ms2 + EPS)
        for half, kind in enumerate(kinds):
            blk = c * blocks_per_chunk + half
            p = p2[:, half * LANES:(half + 1) * LANES]
            if kind != _V:
                t = p * inv2[:, half * LANES:(half + 1) * LANES] * gains_ref[kind:kind + 1, :]
                p = t * cos + pltpu.roll(t, HEAD_DIM, axis=1) * sin
            slab, j = divmod(blk, _PROJ_BLOCKS)
            if slab < 2:
                out, emit = ((oa_ref, _A_EMIT), (o0_ref, _B_EMIT))[slab]
                for dest, mask in emit[j]:
                    out[tok, dest * LANES:(dest + 1) * LANES] = masked(p, mask)
            else:
                stage(slab, j, p)


def _qkv_call(x2d, seq_len, n1, w_qkv, gains, cos, sin, mean_mat):
    tokens = x2d.shape[0]
    tm = QKV_TILE
    sub_tm = tm // QKV_SUBTILES
    tiles_per_seq = seq_len // tm
    sub = DIL_TILE // tm
    n_dil = tokens // DIL_TILE
    d1, d2 = B_GROUPS[1][1], B_GROUPS[2][1]
    const = lambda i: (0, 0)
    tok_sds = jax.ShapeDtypeStruct((tokens, SLAB), _BF)
    return pl.pallas_call(
        _qkv_kernel,
        grid=(tokens // tm,),
        in_specs=[
            pl.BlockSpec((tm, D_MODEL), lambda i: (i, 0)),
            pl.BlockSpec((1, D_MODEL), const),
            pl.BlockSpec((D_MODEL, QKV_W), const, pipeline_mode=pl.Buffered(1)),
            pl.BlockSpec((4, LANES), const),
            pl.BlockSpec((tm, LANES), lambda i: (i % tiles_per_seq, 0)),
            pl.BlockSpec((tm, LANES), lambda i: (i % tiles_per_seq, 0)),
            pl.BlockSpec((MXU_N, MXU_N), const),
        ],
        out_specs=[
            pl.BlockSpec((tm, SLAB), lambda i: (i, 0)),
            pl.BlockSpec((tm, SLAB), lambda i: (i, 0)),
            pl.BlockSpec((None, d1, tm // d1, SLAB), lambda i: (i // sub, 0, i % sub, 0)),
            pl.BlockSpec((None, d2, tm // d2, SLAB), lambda i: (i // sub, 0, i % sub, 0)),
        ],
        out_shape=[
            tok_sds, tok_sds,
            jax.ShapeDtypeStruct((n_dil, d1, DIL_TILE // d1, SLAB), _BF),
            jax.ShapeDtypeStruct((n_dil, d2, DIL_TILE // d2, SLAB), _BF),
        ],
        scratch_shapes=[
            pltpu.VMEM((QKV_SUBTILES, _PROJ_BLOCKS, sub_tm, LANES), _F32),
            pltpu.VMEM((QKV_SUBTILES, _PROJ_BLOCKS, d2 * (sub_tm // d2 + SUBLANES), LANES), _F32)],
        compiler_params=pltpu.CompilerParams(
            dimension_semantics=("arbitrary",), vmem_limit_bytes=VMEM_LIMIT),
        name="qkv_proj",
    )(x2d, n1, w_qkv, gains, cos, sin, mean_mat)


def _pick_bias(bias_ref, first, last):
    if first is None and last is None:
        return bias_ref[0]
    if last is None:
        return jnp.where(first, bias_ref[1], bias_ref[0])
    if first is None:
        return jnp.where(last, bias_ref[2], bias_ref[0])
    return jnp.where(first, jnp.where(last, bias_ref[3], bias_ref[1]),
                     jnp.where(last, bias_ref[2], bias_ref[0]))


def _window_kernel(cur_ref, prev_ref, next_ref, bias_ref, onehot_ref, sink_ref, o_ref, kv_ext,
                   *, seq_len):
    tq = cur_ref.shape[0]
    t = pl.program_id(1)
    kv_ext[0:A_WINDOW, :] = prev_ref[...]
    kv_ext[A_WINDOW:A_WINDOW + tq, :] = cur_ref[:, 0:KV_W]
    kv_ext[A_WINDOW + tq:, :] = next_ref[...]

    masks = _lane_rows(_BF)
    first_half = lax.broadcasted_iota(jnp.int32, (1, LANES), 1) < HEAD_DIM
    ones_lo = jnp.broadcast_to(masks[_LO], (A_BK, LANES))
    ones_hi = jnp.broadcast_to(masks[_HI], (A_BK, LANES))
    onehot = onehot_ref[...]
    n_pairs = A_Q_W // LANES

    n_blocks = tq // BQ
    for i in range(n_blocks):
        row0 = i * BQ
        start = t * tq + row0
        bias = _pick_bias(bias_ref,
                          start - A_WINDOW < 0 if i == 0 else None,
                          start + BQ >= seq_len if i == n_blocks - 1 else None)
        win = pl.ds(row0, A_BK)
        k_cat = jnp.concatenate([kv_ext[win, 0:LANES], kv_ext[win, LANES:2 * LANES]], axis=0)
        k_ext = jnp.concatenate([k_cat, bias], axis=1)
        v_cat = jnp.concatenate(
            [jnp.concatenate([kv_ext[win, 2 * LANES:3 * LANES], ones_lo], axis=1),
             jnp.concatenate([kv_ext[win, 3 * LANES:4 * LANES], ones_hi], axis=1)], axis=0)
        q_cat = jnp.concatenate(
            [cur_ref[pl.ds(row0, BQ), KV_W + g * LANES:KV_W + (g + 1) * LANES]
             for g in range(n_pairs)], axis=0)
        q_ext = jnp.concatenate([q_cat, onehot], axis=1)
        s2 = lax.dot_general(q_ext, k_ext, _NT, preferred_element_type=_F32)
        ps, ms = [], []
        for hh in range(2):
            s = s2[:, hh * A_BK:(hh + 1) * A_BK]
            m = jnp.max(s, axis=-1, keepdims=True)
            ms.append(m)
            ps.append(jnp.exp2(s - m).astype(_BF))
        res = jnp.dot(jnp.concatenate(ps, axis=1), v_cat, preferred_element_type=_F32)
        m_blk = jnp.where(first_half, ms[0], ms[1])
        for g in range(n_pairs):
            rows = slice(g * BQ, (g + 1) * BQ)
            sink = sink_ref[:, g * LANES:(g + 1) * LANES]
            m_new = jnp.maximum(sink, m_blk[rows])
            a = jnp.exp2(sink - m_new)
            b = jnp.exp2(m_blk[rows] - m_new)
            out = b * res[rows, 0:LANES] / (a + b * res[rows, LANES:2 * LANES])
            o_ref[pl.ds(row0, BQ), g * LANES:(g + 1) * LANES] = out.astype(_BF)


def _window_call(qkv, sink):
    batch, seq, _ = qkv.shape
    tq = A_TILE
    hb = tq // A_WINDOW
    last_halo = seq // A_WINDOW - 1
    n_pairs = A_Q_W // LANES
    const2 = lambda b, t: (0, 0)
    return pl.pallas_call(
        functools.partial(_window_kernel, seq_len=seq),
        grid=(batch, seq // tq),
        in_specs=[
            pl.BlockSpec((None, tq, SLAB), lambda b, t: (b, t, 0)),
            pl.BlockSpec((None, A_WINDOW, KV_W), lambda b, t: (b, jnp.maximum(t * hb - 1, 0), 0)),
            pl.BlockSpec((None, A_WINDOW, KV_W),
                         lambda b, t: (b, jnp.minimum((t + 1) * hb, last_halo), 0)),
            pl.BlockSpec((4, 2 * A_BK, BQ), lambda b, t: (0, 0, 0)),
            pl.BlockSpec((n_pairs * BQ, BQ), const2),
            pl.BlockSpec((1, A_Q_W), const2),
        ],
        out_specs=pl.BlockSpec((None, tq, A_Q_W), lambda b, t: (b, t, 0)),
        out_shape=jax.ShapeDtypeStruct((batch, seq, A_Q_W), _BF),
        scratch_shapes=[pltpu.VMEM((tq + 2 * A_WINDOW, KV_W), _BF)],
        compiler_params=pltpu.CompilerParams(
            dimension_semantics=("arbitrary",) * 2, vmem_limit_bytes=VMEM_LIMIT),
        name="attn_window",
    )(qkv, qkv, qkv, _bias_variants(A_WINDOW, 2), _one_hot_rows(n_pairs * BQ), sink)


def _dilated_kernel(c0, p0, n0, c1, p1, n1, c2, p2, n2, bias_ref, onehot_ref, o_ref,
                    ext0, ext1, lse_ref, acc_ref, *, seq_len):
    t = pl.program_id(1)
    d1, d2 = B_GROUPS[1][1], B_GROUPS[2][1]
    rows0, rows1 = DIL_TILE, DIL_TILE // d1
    n_pairs = B_OUT_W // LANES
    masks = _lane_rows(_BF)
    k_w = n_pairs * LANES

    def spread(src):
        parts = [src[:, 0:k_w]]
        for pair in range(n_pairs):
            v = src[:, k_w + pair * LANES:k_w + (pair + 1) * LANES]
            parts += [v * masks[_LO], v * masks[_HI]]
        return jnp.concatenate(parts, axis=1)

    ext0[0:B_HALO, :] = spread(p0[...])
    ext0[B_HALO:B_HALO + rows0, :] = spread(c0[:, 0:KV_W])
    ext0[B_HALO + rows0:, :] = spread(n0[...])
    for r in range(d1):
        ext1[r, 0:B_HALO, :] = spread(p1[r])
        ext1[r, B_HALO:B_HALO + rows1, :] = spread(c1[r, :, 0:KV_W])
        ext1[r, B_HALO + rows1:, :] = spread(n1[r])

    first_half = lax.broadcasted_iota(jnp.int32, (1, LANES), 1) < HEAD_DIM
    ones_lo = jnp.broadcast_to(masks[_LO], (B_BK, LANES))
    ones_hi = jnp.broadcast_to(masks[_HI], (B_BK, LANES))
    onehot = onehot_ref[...]

    def bias_for(pos0, limit, i, n_blocks):
        return _pick_bias(bias_ref, pos0 < 0 if i == 0 else None,
                          pos0 + B_BK > limit if i == n_blocks - 1 else None)

    def unit(q2, kw, v_lo, v_hi, bias):
        s = lax.dot_general(jnp.concatenate([q2, onehot], axis=1),
                            jnp.concatenate([kw, bias], axis=1), _NT,
                            preferred_element_type=_F32)
        m = jnp.max(s, axis=-1, keepdims=True)
        p = jnp.exp2(s - m).astype(_BF)
        p_cat = jnp.concatenate([p[0:BQ], p[BQ:2 * BQ]], axis=1)
        v_cat = jnp.concatenate([jnp.concatenate([v_lo, ones_lo], axis=1),
                                 jnp.concatenate([v_hi, ones_hi], axis=1)], axis=0)
        res = jnp.dot(p_cat, v_cat, preferred_element_type=_F32)
        return jnp.where(first_half, m[0:BQ], m[BQ:2 * BQ]), res[:, LANES:], res[:, 0:LANES]

    def stacked_q(block):
        return jnp.concatenate([block[:, 0:LANES], block[:, LANES:2 * LANES]], axis=0)

    def merge(pair, rows, m_blk, l_blk, pv, last):
        lse_old = lse_ref[pair, rows, :]
        m_new = jnp.maximum(lse_old, m_blk)
        a = jnp.exp2(lse_old - m_new)
        b = jnp.exp2(m_blk - m_new)
        l_new = a + b * l_blk
        acc_ref[pair, rows, :] = (a * acc_ref[pair, rows, :] + b * pv) / l_new
        if not last:
            lse_ref[pair, rows, :] = m_new + jnp.log2(l_new)

    def ext_unit(q_block, ext_window, pair, bias):
        kw = ext_window[:, pair * LANES:(pair + 1) * LANES]
        v0 = k_w + 2 * pair * LANES
        return unit(stacked_q(q_block), kw, ext_window[:, v0:v0 + LANES],
                    ext_window[:, v0 + LANES:v0 + 2 * LANES], bias)

    def group0(i):
        row0 = i * BQ
        bias = bias_for(t * DIL_TILE + row0 - B_HALO, seq_len, i, rows0 // BQ)
        for pair in range(n_pairs):
            q0 = KV_W + 2 * pair * LANES
            m_blk, l_blk, pv = ext_unit(c0[pl.ds(row0, BQ), q0:q0 + 2 * LANES],
                                        ext0[pl.ds(row0, B_BK), :], pair, bias)
            rows = pl.ds(row0, BQ)
            lse_ref[pair, rows, :] = m_blk + jnp.log2(l_blk)
            acc_ref[pair, rows, :] = pv / l_blk

    def group1(r, i):
        row0 = i * BQ
        bias = bias_for(t * rows1 + row0 - B_HALO, seq_len // d1, i, rows1 // BQ)
        for pair in range(n_pairs):
            q0 = KV_W + 2 * pair * LANES
            m_blk, l_blk, pv = ext_unit(c1[r, pl.ds(row0, BQ), q0:q0 + 2 * LANES],
                                        ext1[r, pl.ds(row0, B_BK), :], pair, bias)
            merge(pair, pl.ds(row0 * d1 + r, BQ, stride=d1), m_blk, l_blk, pv, last=False)

    bias2 = bias_for(t * BQ - B_HALO, seq_len // d2, 0, 1)

    def group2(r):
        for pair in range(n_pairs):
            q0 = KV_W + 2 * pair * LANES
            k_l = pair * LANES
            v_l = k_w + pair * LANES
            kw = jnp.concatenate([p2[r, :, k_l:k_l + LANES], c2[r, :, k_l:k_l + LANES],
                                  n2[r, :, k_l:k_l + LANES]], axis=0)
            vw = jnp.concatenate([p2[r, :, v_l:v_l + LANES], c2[r, :, v_l:v_l + LANES],
                                  n2[r, :, v_l:v_l + LANES]], axis=0)
            m_blk, l_blk, pv = unit(stacked_q(c2[r, :, q0:q0 + 2 * LANES]), kw,
                                    vw * masks[_LO], vw * masks[_HI], bias2)
            merge(pair, pl.ds(r, BQ, stride=d2), m_blk, l_blk, pv, last=True)

    for i in range(rows0 // BQ):
        group0(i)
    for r in range(d1):
        for i in range(rows1 // BQ):
            group1(r, i)
    for r in range(d2):
        group2(r)

    def finish(i, carry):
        rows = pl.ds(pl.multiple_of(i * BQ, BQ), BQ)
        for pair in range(n_pairs):
            o_ref[rows, pair * LANES:(pair + 1) * LANES] = acc_ref[pair, rows, :].astype(_BF)
        return carry

    lax.fori_loop(0, DIL_TILE // BQ, finish, 0)


def _dilated_call(g0, g1, g2, batch, seq):
    n_t = seq // DIL_TILE
    d1, d2 = B_GROUPS[1][1], B_GROUPS[2][1]
    rows1, rows2 = DIL_TILE // d1, DIL_TILE // d2
    hb0 = DIL_TILE // B_HALO
    last0 = seq // B_HALO - 1
    prev_t = lambda t: jnp.maximum(t - 1, 0)
    next_t = lambda t: jnp.minimum(t + 1, n_t - 1)
    in_specs = [
        pl.BlockSpec((None, DIL_TILE, SLAB), lambda b, t: (b, t, 0)),
        pl.BlockSpec((None, B_HALO, KV_W), lambda b, t: (b, jnp.maximum(t * hb0 - 1, 0), 0)),
        pl.BlockSpec((None, B_HALO, KV_W), lambda b, t: (b, jnp.minimum((t + 1) * hb0, last0), 0)),
        pl.BlockSpec((None, None, d1, rows1, SLAB), lambda b, t: (b, t, 0, 0, 0)),
        pl.BlockSpec((None, None, d1, B_HALO, KV_W),
                     lambda b, t: (b, prev_t(t), 0, rows1 // B_HALO - 1, 0)),
        pl.BlockSpec((None, None, d1, B_HALO, KV_W), lambda b, t: (b, next_t(t), 0, 0, 0)),
        pl.BlockSpec((None, None, d2, rows2, SLAB), lambda b, t: (b, t, 0, 0, 0)),
        pl.BlockSpec((None, None, d2, B_HALO, KV_W),
                     lambda b, t: (b, prev_t(t), 0, rows2 // B_HALO - 1, 0)),
        pl.BlockSpec((None, None, d2, B_HALO, KV_W), lambda b, t: (b, next_t(t), 0, 0, 0)),
        pl.BlockSpec((4, B_BK, BQ), lambda b, t: (0, 0, 0)),
        pl.BlockSpec((2 * BQ, BQ), lambda b, t: (0, 0)),
    ]
    n_pairs = B_OUT_W // LANES
    ext_w = n_pairs * 3 * LANES
    state = pltpu.VMEM((n_pairs, DIL_TILE, LANES), _F32)
    return pl.pallas_call(
        functools.partial(_dilated_kernel, seq_len=seq),
        grid=(batch, n_t),
        in_specs=in_specs,
        out_specs=pl.BlockSpec((None, DIL_TILE, B_OUT_W), lambda b, t: (b, t, 0)),
        out_shape=jax.ShapeDtypeStruct((batch, seq, B_OUT_W), _BF),
        scratch_shapes=[
            pltpu.VMEM((DIL_TILE + 2 * B_HALO, ext_w), _BF),
            pltpu.VMEM((d1, rows1 + 2 * B_HALO, ext_w), _BF),
            state, state,
        ],
        compiler_params=pltpu.CompilerParams(
            dimension_semantics=("arbitrary",) * 2, vmem_limit_bytes=VMEM_LIMIT),
        name="attn_dilated",
    )(g0, g0, g0, g1, g1, g1, g2, g2, g2, _bias_variants(B_HALO, 1), _one_hot_rows(2 * BQ))


def _out_kernel(x_ref, oa_ref, ob_ref, n1_ref, wga_ref, wgb_ref, wa_ref, wb_ref, wo_ref,
                n2_ref, wg_ref, wu_ref, wd_ref, y_ref, mix_ref, act_ref):
    x = x_ref[...]
    h1 = _rms_rows(x, n1_ref[...]).astype(_BF)
    oa = oa_ref[...]
    ob = ob_ref[...]
    for c in range(D_MODEL // MXU_N):
        cs = slice(c * MXU_N, (c + 1) * MXU_N)
        ga = jax.nn.sigmoid(jnp.dot(h1, wga_ref[:, cs], preferred_element_type=_F32))
        gb = jax.nn.sigmoid(jnp.dot(h1, wgb_ref[:, cs], preferred_element_type=_F32))
        ya = jnp.dot(oa, wa_ref[:, cs], preferred_element_type=_F32)
        yb = jnp.dot(ob, wb_ref[:, cs], preferred_element_type=_F32)
        mix_ref[:, cs] = (ga * ya + gb * yb).astype(_BF)
    x1 = x + jnp.dot(mix_ref[...], wo_ref[...], preferred_element_type=_F32)
    h2 = _rms_rows(x1, n2_ref[...]).astype(_BF)
    for c in range(D_FF // MXU_N):
        cs = slice(c * MXU_N, (c + 1) * MXU_N)
        gate = jnp.dot(h2, wg_ref[:, cs], preferred_element_type=_F32)
        up = jnp.dot(h2, wu_ref[:, cs], preferred_element_type=_F32)
        act_ref[:, cs] = (jax.nn.silu(gate) * up).astype(_BF)
    y_ref[...] = x1 + jnp.dot(act_ref[...], wd_ref[...], preferred_element_type=_F32)


def _out_call(x2d, oa, ob, n1, wga, wgb, wa, wb, wo, n2, wg, wu, wd):
    tokens = x2d.shape[0]
    tm = TOKEN_TILE
    row = lambda i: (i, 0)
    const = lambda i: (0, 0)
    once = dict(pipeline_mode=pl.Buffered(1))
    return pl.pallas_call(
        _out_kernel,
        grid=(tokens // tm,),
        in_specs=[
            pl.BlockSpec((tm, D_MODEL), row),
            pl.BlockSpec((tm, A_Q_W), row),
            pl.BlockSpec((tm, B_OUT_W), row),
            pl.BlockSpec((1, D_MODEL), const),
            pl.BlockSpec((D_MODEL, D_MODEL), const, **once),
            pl.BlockSpec((D_MODEL, D_MODEL), const, **once),
            pl.BlockSpec((A_Q_W, D_MODEL), const, **once),
            pl.BlockSpec((B_OUT_W, D_MODEL), const, **once),
            pl.BlockSpec((D_MODEL, D_MODEL), const, **once),
            pl.BlockSpec((1, D_MODEL), const),
            pl.BlockSpec((D_MODEL, D_FF), const, **once),
            pl.BlockSpec((D_MODEL, D_FF), const, **once),
            pl.BlockSpec((D_FF, D_MODEL), const, **once),
        ],
        out_specs=pl.BlockSpec((tm, D_MODEL), row),
        out_shape=jax.ShapeDtypeStruct((tokens, D_MODEL), _F32),
        scratch_shapes=[pltpu.VMEM((tm, D_MODEL), _BF), pltpu.VMEM((tm, D_FF), _BF)],
        compiler_params=pltpu.CompilerParams(
            dimension_semantics=("arbitrary",), vmem_limit_bytes=VMEM_LIMIT),
        name="out_ffn",
    )(x2d, oa, ob, n1, wga, wgb, wa, wb, wo, n2, wg, wu, wd)


def _prepare_weights(norm1, w_in, qn_a, kn_a, sink_a, qn_b, kn_b, w_br_a, w_br_b, w_out,
                     norm2, w_gate, w_up, w_down):
    w_in0 = w_in[0]
    gate0 = QKV_W
    dims = _pair_lane_dims()
    gains = jnp.stack([qn_a[0][dims] * Q_SCALE, kn_a[0][dims],
                       qn_b[0][dims] * Q_SCALE, kn_b[0][dims]]).astype(_F32)
    sink = sink_a[0].astype(_F32) * math.log2(math.e)
    half_heads = A_Q_HEADS // 2
    sink_pairs = jnp.stack([jnp.repeat(sink[:half_heads], HEAD_DIM).reshape(half_heads, HEAD_DIM),
                            jnp.repeat(sink[half_heads:], HEAD_DIM).reshape(half_heads, HEAD_DIM)],
                           axis=1).reshape(1, A_Q_W)
    return dict(
        n1=norm1[0].reshape(1, D_MODEL).astype(_F32),
        w_qkv=w_in0[:, _qkv_columns()].astype(_BF),
        gains=gains,
        sink=sink_pairs,
        wga=w_in0[:, gate0:gate0 + D_MODEL].astype(_BF),
        wgb=w_in0[:, gate0 + D_MODEL:gate0 + 2 * D_MODEL].astype(_BF),
        wa=w_br_a[0][_oa_rows(), :].astype(_BF),
        wb=w_br_b[0].astype(_BF),
        wo=w_out[0].astype(_BF),
        n2=norm2[0].reshape(1, D_MODEL).astype(_F32),
        wg=w_gate[0].astype(_BF),
        wu=w_up[0].astype(_BF),
        wd=w_down[0].astype(_BF),
    )


def _trunk(x, wts, mean_mat, cos, sin):
    batch, seq, _ = x.shape
    assert seq % DIL_TILE == 0 and seq >= 2 * BQ and seq <= cos.shape[0]
    tokens = batch * seq
    n_t = seq // DIL_TILE
    x2d = x.reshape(tokens, D_MODEL)
    qkv_a, g0, g1, g2 = _qkv_call(x2d, seq, wts["n1"], wts["w_qkv"], wts["gains"], cos, sin,
                                  mean_mat)
    oa = _window_call(qkv_a.reshape(batch, seq, SLAB), wts["sink"])
    ob = _dilated_call(g0.reshape(batch, seq, SLAB),
                       g1.reshape((batch, n_t) + g1.shape[1:]),
                       g2.reshape((batch, n_t) + g2.shape[1:]), batch, seq)
    y = _out_call(x2d, oa.reshape(tokens, A_Q_W), ob.reshape(tokens, B_OUT_W),
                  wts["n1"], wts["wga"], wts["wgb"], wts["wa"], wts["wb"], wts["wo"],
                  wts["n2"], wts["wg"], wts["wu"], wts["wd"])
    return y.reshape(batch, seq, D_MODEL)


def kernel(x_prompt, x_sample, norm1, w_in, qn_a, kn_a, sink_a, qn_b, kn_b, w_br_a, w_br_b,
           w_out, norm2, w_gate, w_up, w_down):
    wts = _prepare_weights(norm1, w_in, qn_a, kn_a, sink_a, qn_b, kn_b, w_br_a, w_br_b, w_out,
                           norm2, w_gate, w_up, w_down)
    mean_mat = _head_mean_matrix()
    cos, sin = _rope_tables(max(x_prompt.shape[1], x_sample.shape[1]))
    return (_trunk(x_prompt, wts, mean_mat, cos, sin), _trunk(x_sample, wts, mean_mat, cos, sin))
```

```python
import functools
import math

import numpy as np
import jax
import jax.numpy as jnp
from jax import lax
from jax.experimental import pallas as pl
from jax.experimental.pallas import tpu as pltpu

D_MODEL = 1024
HEAD_DIM = 64
HALF = HEAD_DIM // 2
A_Q_HEADS = 8
A_KV_HEADS = 2
A_WINDOW = 128
B_GROUPS = ((128, 1), (512, 4), (2048, 16))
B_HEADS_PER_GROUP = 4
B_HEADS = B_HEADS_PER_GROUP * len(B_GROUPS)
D_FF = 2816
ROPE_THETA = 10000.0
EPS = 1e-6
NEG_INF = -1e30

A_Q_W = A_Q_HEADS * HEAD_DIM
A_KV_W = A_KV_HEADS * HEAD_DIM
B_W = B_HEADS * HEAD_DIM
B_OUT_W = B_HEADS_PER_GROUP * HEAD_DIM
PROJ_SLAB = A_Q_W + 2 * A_KV_W
QKV_W = PROJ_SLAB * (1 + len(B_GROUPS))
assert PROJ_SLAB == 3 * B_OUT_W

LANES = 128
SUBLANES = 8
MXU_N = 256
VMEM_LIMIT = 56 * 1024 * 1024
BQ = 128
B_HALO = 64
B_BK = BQ + 2 * B_HALO
A_BK = BQ + 2 * A_WINDOW
DIL_TILE = 2048
TOKEN_TILE = 512
QKV_TILE = 1024
QKV_SUBTILES = 2
A_TILE = 4096
assert all(w // (2 * d) == B_HALO for w, d in B_GROUPS)
assert DIL_TILE // B_GROUPS[-1][1] == BQ
Q_SCALE = HEAD_DIM ** -0.5 * math.log2(math.e)

KV_W = 4 * LANES
SLAB = 8 * LANES
_NONE, _H0, _H1, _LO, _HI = range(5)
_A_EMIT = ([[(4 + j, _NONE)] for j in range(4)]
           + [[(0, _H0), (1, _H1)], [(2, _LO), (3, _HI)]])
_B_EMIT = ([[(4 + 2 * j, _H0), (5 + 2 * j, _H1)] for j in range(2)]
           + [[(j, _NONE)] for j in range(2)] + [[(2 + j, _NONE)] for j in range(2)])

_BF = jnp.bfloat16
_F32 = jnp.float32
_NT = (((1,), (1,)), ((), ()))


def _pair_lane_dims():
    lane = np.arange(LANES)
    return (lane // HEAD_DIM) * HALF + lane % HALF


def _pair_cols(h0, h1, base):
    lane = np.arange(LANES)
    head = np.where((lane // HALF) % 2 == 0, h0, h1)
    return base + head * HEAD_DIM + _pair_lane_dims()


def _qkv_columns():
    cols = []
    q_base, k_base, v_base = 0, A_Q_W, A_Q_W + A_KV_W
    for j in range(A_Q_HEADS // 2):
        cols.append(_pair_cols(j, j + A_Q_HEADS // 2, q_base))
    cols.append(_pair_cols(0, 1, k_base))
    cols.append(np.arange(v_base, v_base + A_KV_W))
    qb, kb, vb = PROJ_SLAB, PROJ_SLAB + B_W, PROJ_SLAB + 2 * B_W
    for g in range(len(B_GROUPS)):
        h = g * B_HEADS_PER_GROUP
        cols += [_pair_cols(h, h + 1, qb), _pair_cols(h + 2, h + 3, qb)]
        cols += [_pair_cols(h, h + 1, kb), _pair_cols(h + 2, h + 3, kb)]
        cols.append(np.arange(vb + h * HEAD_DIM, vb + (h + 4) * HEAD_DIM))
    return np.concatenate(cols)


_Q_A, _K_A, _Q_B, _K_B, _V = range(5)
_BLOCK_KINDS = ([_Q_A] * 4 + [_K_A, _V]) + ([_Q_B] * 2 + [_K_B] * 2 + [_V] * 2) * len(B_GROUPS)
_PROJ_BLOCKS = PROJ_SLAB // LANES


def _oa_rows():
    rows = []
    for j in range(A_Q_HEADS // 2):
        rows.append(np.arange(j * HEAD_DIM, (j + 1) * HEAD_DIM))
        rows.append(np.arange((j + 4) * HEAD_DIM, (j + 5) * HEAD_DIM))
    return np.concatenate(rows)


def _rope_tables(seq_len):
    lane = np.arange(LANES)
    inv_freq = ROPE_THETA ** (-jnp.arange(0, HEAD_DIM, 2, dtype=_F32) / HEAD_DIM)
    ang = jnp.arange(seq_len, dtype=_F32)[:, None] * inv_freq[lane % HALF][None, :]
    sign = jnp.where(lane < HEAD_DIM, -1.0, 1.0).astype(_F32)
    return jnp.cos(ang), jnp.sin(ang) * sign[None, :]


def _head_mean_matrix():
    lane = np.arange(MXU_N)
    head = lane // HALF % 2 + 2 * (lane // LANES)
    same = (head[:, None] == head[None, :]).astype(np.float32) / HEAD_DIM
    return jnp.asarray(same, dtype=_BF)


def _one_hot_rows(n_rows):
    return jnp.asarray(np.arange(n_rows)[:, None] % BQ == np.arange(BQ)[None, :], dtype=_BF)


def _bias_variants(halo, copies):
    bk = BQ + 2 * halo
    c = np.arange(bk)[:, None]
    a = np.arange(BQ)[None, :]
    band = (c >= a) & (c <= a + 2 * halo)
    out = []
    for first, last in ((0, 0), (1, 0), (0, 1), (1, 1)):
        keep = band & ((c >= halo) | (not first)) & ((c < bk - halo) | (not last))
        out.append(np.tile(np.where(keep, 0.0, NEG_INF).astype(np.float32), (copies, 1)))
    return jnp.asarray(np.stack(out), dtype=_BF)


def _lane_rows(dtype):
    lane = lax.broadcasted_iota(jnp.int32, (1, LANES), 1)
    head1 = (lane // HALF) % 2 == 1
    hi = lane >= HEAD_DIM
    as_row = lambda m: m.astype(_F32).astype(dtype)
    return [None, as_row(~head1), as_row(head1), as_row(~hi), as_row(hi)]


def _rms_rows(x, gain):
    ms = jnp.mean(x * x, axis=-1, keepdims=True)
    return x * lax.rsqrt(ms + EPS) * gain


def _qkv_kernel(x_ref, n1_ref, w_ref, gains_ref, cos_ref, sin_ref, mean_ref,
                oa_ref, o0_ref, o1_ref, o2_ref, stage_ref, fold_ref):
    tm = x_ref.shape[0] // QKV_SUBTILES
    for s in range(QKV_SUBTILES):
        _qkv_subtile(s, tm, x_ref, n1_ref, w_ref, gains_ref, cos_ref, sin_ref, mean_ref,
                     oa_ref, o0_ref, o1_ref, o2_ref, stage_ref.at[s], fold_ref.at[s])


def _qkv_subtile(s, tm, x_ref, n1_ref, w_ref, gains_ref, cos_ref, sin_ref, mean_ref,
                 oa_ref, o0_ref, o1_ref, o2_ref, stage_ref, fold_ref):
    tok = slice(s * tm, (s + 1) * tm)
    h = _rms_rows(x_ref[tok, :], n1_ref[...]).astype(_BF)
    cos = cos_ref[tok, :]
    sin = sin_ref[tok, :]
    masks = _lane_rows(_F32)
    blocks_per_chunk = MXU_N // LANES
    n_chunks = QKV_W // MXU_N

    def project(c):
        return jnp.dot(h, w_ref[:, c * MXU_N:(c + 1) * MXU_N], preferred_element_type=_F32)

    def masked(p, mask):
        return (p if mask == _NONE else p * masks[mask]).astype(_BF)

    d1, d2 = B_GROUPS[1][1], B_GROUPS[2][1]
    pitch = fold_ref.shape[1] // d2

    def stage(slab, j, p):
        if slab == 2:
            stage_ref[j] = p
        else:
            for g8 in range(tm // 8):
                row, r0 = divmod(g8 * 8, d2)
                fold_ref[j, pl.ds(r0 * pitch + row, 8, stride=pitch), :] = p[g8 * 8:(g8 + 1) * 8]

    def unfold(slab, out):
        d = d1 if slab == 2 else d2
        for j in range(_PROJ_BLOCKS):
            for r in range(d):
                if slab == 2:
                    rows = stage_ref[j, pl.ds(r, tm // d, stride=d), :]
                else:
                    rows = fold_ref[j, r * pitch:r * pitch + tm // d, :]
                for dest, mask in _B_EMIT[j]:
                    out[r, s * (tm // d):(s + 1) * (tm // d),
                        dest * LANES:(dest + 1) * LANES] = masked(rows, mask)

    chunks_per_slab = PROJ_SLAB // MXU_N
    order = [s * chunks_per_slab + k for s in (3, 2, 0, 1) for k in range(chunks_per_slab)]
    p2_next = project(order[0])
    for n, c in enumerate(order):
        kinds = _BLOCK_KINDS[c * blocks_per_chunk:(c + 1) * blocks_per_chunk]
        p2 = p2_next
        if n + 1 < n_chunks:
            p2_next = project(order[n + 1])
        if n == chunks_per_slab:
            unfold(3, o2_ref)
        if n == 2 * chunks_per_slab:
            unfold(2, o1_ref)
        if any(k != _V for k in kinds):
            ms2 = jnp.dot((p2 * p2).astype(_BF), mean_ref[...], preferred_element_type=_F32)
            inv2 = lax.rsqrt(ms2 + EPS)
        for half, kind in enumerate(kinds):
            blk = c * blocks_per_chunk + half
            p = p2[:, half * LANES:(half + 1) * LANES]
            if kind != _V:
                t = p * inv2[:, half * LANES:(half + 1) * LANES] * gains_ref[kind:kind + 1, :]
                p = t * cos + pltpu.roll(t, HEAD_DIM, axis=1) * sin
            slab, j = divmod(blk, _PROJ_BLOCKS)
            if slab < 2:
                out, emit = ((oa_ref, _A_EMIT), (o0_ref, _B_EMIT))[slab]
                for dest, mask in emit[j]:
                    out[tok, dest * LANES:(dest + 1) * LANES] = masked(p, mask)
            else:
                stage(slab, j, p)


def _qkv_call(x2d, seq_len, n1, w_qkv, gains, cos, sin, mean_mat):
    tokens = x2d.shape[0]
    tm = QKV_TILE
    sub_tm = tm // QKV_SUBTILES
    tiles_per_seq = seq_len // tm
    sub = DIL_TILE // tm
    n_dil = tokens // DIL_TILE
    d1, d2 = B_GROUPS[1][1], B_GROUPS[2][1]
    const = lambda i: (0, 0)
    tok_sds = jax.ShapeDtypeStruct((tokens, SLAB), _BF)
    return pl.pallas_call(
        _qkv_kernel,
        grid=(tokens // tm,),
        in_specs=[
            pl.BlockSpec((tm, D_MODEL), lambda i: (i, 0)),
            pl.BlockSpec((1, D_MODEL), const),
            pl.BlockSpec((D_MODEL, QKV_W), const, pipeline_mode=pl.Buffered(1)),
            pl.BlockSpec((4, LANES), const),
            pl.BlockSpec((tm, LANES), lambda i: (i % tiles_per_seq, 0)),
            pl.BlockSpec((tm, LANES), lambda i: (i % tiles_per_seq, 0)),
            pl.BlockSpec((MXU_N, MXU_N), const),
        ],
        out_specs=[
            pl.BlockSpec((tm, SLAB), lambda i: (i, 0)),
            pl.BlockSpec((tm, SLAB), lambda i: (i, 0)),
            pl.BlockSpec((None, d1, tm // d1, SLAB), lambda i: (i // sub, 0, i % sub, 0)),
            pl.BlockSpec((None, d2, tm // d2, SLAB), lambda i: (i // sub, 0, i % sub, 0)),
        ],
        out_shape=[
            tok_sds, tok_sds,
            jax.ShapeDtypeStruct((n_dil, d1, DIL_TILE // d1, SLAB), _BF),
            jax.ShapeDtypeStruct((n_dil, d2, DIL_TILE // d2, SLAB), _BF),
        ],
        scratch_shapes=[
            pltpu.VMEM((QKV_SUBTILES, _PROJ_BLOCKS, sub_tm, LANES), _F32),
            pltpu.VMEM((QKV_SUBTILES, _PROJ_BLOCKS, d2 * (sub_tm // d2 + SUBLANES), LANES), _F32)],
        compiler_params=pltpu.CompilerParams(
            dimension_semantics=("arbitrary",), vmem_limit_bytes=VMEM_LIMIT),
        name="qkv_proj",
    )(x2d, n1, w_qkv, gains, cos, sin, mean_mat)


def _pick_bias(bias_ref, first, last):
    if first is None and last is None:
        return bias_ref[0]
    if last is None:
        return jnp.where(first, bias_ref[1], bias_ref[0])
    if first is None:
        return jnp.where(last, bias_ref[2], bias_ref[0])
    return jnp.where(first, jnp.where(last, bias_ref[3], bias_ref[1]),
                     jnp.where(last, bias_ref[2], bias_ref[0]))


def _window_kernel(cur_ref, prev_ref, next_ref, bias_ref, onehot_ref, sink_ref, o_ref, kv_ext,
                   *, seq_len):
    tq = cur_ref.shape[0]
    t = pl.program_id(1)
    kv_ext[0:A_WINDOW, :] = prev_ref[...]
    kv_ext[A_WINDOW:A_WINDOW + tq, :] = cur_ref[:, 0:KV_W]
    kv_ext[A_WINDOW + tq:, :] = next_ref[...]

    masks = _lane_rows(_BF)
    first_half = lax.broadcasted_iota(jnp.int32, (1, LANES), 1) < HEAD_DIM
    ones_lo = jnp.broadcast_to(masks[_LO], (A_BK, LANES))
    ones_hi = jnp.broadcast_to(masks[_HI], (A_BK, LANES))
    onehot = onehot_ref[...]
    n_pairs = A_Q_W // LANES

    n_blocks = tq // BQ
    for i in range(n_blocks):
        row0 = i * BQ
        start = t * tq + row0
        bias = _pick_bias(bias_ref,
                          start - A_WINDOW < 0 if i == 0 else None,
                          start + BQ >= seq_len if i == n_blocks - 1 else None)
        win = pl.ds(row0, A_BK)
        k_cat = jnp.concatenate([kv_ext[win, 0:LANES], kv_ext[win, LANES:2 * LANES]], axis=0)
        k_ext = jnp.concatenate([k_cat, bias], axis=1)
        v_cat = jnp.concatenate(
            [jnp.concatenate([kv_ext[win, 2 * LANES:3 * LANES], ones_lo], axis=1),
             jnp.concatenate([kv_ext[win, 3 * LANES:4 * LANES], ones_hi], axis=1)], axis=0)
        q_cat = jnp.concatenate(
            [cur_ref[pl.ds(row0, BQ), KV_W + g * LANES:KV_W + (g + 1) * LANES]
             for g in range(n_pairs)], axis=0)
        q_ext = jnp.concatenate([q_cat, onehot], axis=1)
        s2 = lax.dot_general(q_ext, k_ext, _NT, preferred_element_type=_F32)
        ps, ms = [], []
        for hh in range(2):
            s = s2[:, hh * A_BK:(hh + 1) * A_BK]
            m = jnp.max(s, axis=-1, keepdims=True)
            ms.append(m)
            ps.append(jnp.exp2(s - m).astype(_BF))
        res = jnp.dot(jnp.concatenate(ps, axis=1), v_cat, preferred_element_type=_F32)
        m_blk = jnp.where(first_half, ms[0], ms[1])
        for g in range(n_pairs):
            rows = slice(g * BQ, (g + 1) * BQ)
            sink = sink_ref[:, g * LANES:(g + 1) * LANES]
            m_new = jnp.maximum(sink, m_blk[rows])
            a = jnp.exp2(sink - m_new)
            b = jnp.exp2(m_blk[rows] - m_new)
            out = b * res[rows, 0:LANES] / (a + b * res[rows, LANES:2 * LANES])
            o_ref[pl.ds(row0, BQ), g * LANES:(g + 1) * LANES] = out.astype(_BF)


def _window_call(qkv, sink):
    batch, seq, _ = qkv.shape
    tq = A_TILE
    hb = tq // A_WINDOW
    last_halo = seq // A_WINDOW - 1
    n_pairs = A_Q_W // LANES
    const2 = lambda b, t: (0, 0)
    return pl.pallas_call(
        functools.partial(_window_kernel, seq_len=seq),
        grid=(batch, seq // tq),
        in_specs=[
            pl.BlockSpec((None, tq, SLAB), lambda b, t: (b, t, 0)),
            pl.BlockSpec((None, A_WINDOW, KV_W), lambda b, t: (b, jnp.maximum(t * hb - 1, 0), 0)),
            pl.BlockSpec((None, A_WINDOW, KV_W),
                         lambda b, t: (b, jnp.minimum((t + 1) * hb, last_halo), 0)),
            pl.BlockSpec((4, 2 * A_BK, BQ), lambda b, t: (0, 0, 0)),
            pl.BlockSpec((n_pairs * BQ, BQ), const2),
            pl.BlockSpec((1, A_Q_W), const2),
        ],
        out_specs=pl.BlockSpec((None, tq, A_Q_W), lambda b, t: (b, t, 0)),
        out_shape=jax.ShapeDtypeStruct((batch, seq, A_Q_W), _BF),
        scratch_shapes=[pltpu.VMEM((tq + 2 * A_WINDOW, KV_W), _BF)],
        compiler_params=pltpu.CompilerParams(
            dimension_semantics=("arbitrary",) * 2, vmem_limit_bytes=VMEM_LIMIT),
        name="attn_window",
    )(qkv, qkv, qkv, _bias_variants(A_WINDOW, 2), _one_hot_rows(n_pairs * BQ), sink)


def _dilated_kernel(c0, p0, n0, c1, p1, n1, c2, p2, n2, bias_ref, onehot_ref, o_ref,
                    lse_ref, acc_ref, *, seq_len):
    t = pl.program_id(1)
    d1, d2 = B_GROUPS[1][1], B_GROUPS[2][1]
    rows0, rows1 = DIL_TILE, DIL_TILE // d1
    n_pairs = B_OUT_W // LANES
    masks = _lane_rows(_BF)
    k_w = n_pairs * LANES

    def window(cur, prev, nxt, i, lanes):
        n_rows = cur.shape[0]
        lo, hi = i * BQ - B_HALO, i * BQ + BQ + B_HALO
        parts = [prev[:, lanes]] if lo < 0 else []
        parts.append(cur[max(lo, 0):min(hi, n_rows), lanes])
        if hi > n_rows:
            parts.append(nxt[:, lanes])
        return parts[0] if len(parts) == 1 else jnp.concatenate(parts, axis=0)

    first_half = lax.broadcasted_iota(jnp.int32, (1, LANES), 1) < HEAD_DIM
    ones_lo = jnp.broadcast_to(masks[_LO], (B_BK, LANES))
    ones_hi = jnp.broadcast_to(masks[_HI], (B_BK, LANES))
    onehot = onehot_ref[...]

    def bias_for(pos0, limit, i, n_blocks):
        return _pick_bias(bias_ref, pos0 < 0 if i == 0 else None,
                          pos0 + B_BK > limit if i == n_blocks - 1 else None)

    def unit(q2, kw, v_lo, v_hi, bias):
        s = lax.dot_general(jnp.concatenate([q2, onehot], axis=1),
                            jnp.concatenate([kw, bias], axis=1), _NT,
                            preferred_element_type=_F32)
        m = jnp.max(s, axis=-1, keepdims=True)
        p = jnp.exp2(s - m).astype(_BF)
        p_cat = jnp.concatenate([p[0:BQ], p[BQ:2 * BQ]], axis=1)
        v_cat = jnp.concatenate([jnp.concatenate([v_lo, ones_lo], axis=1),
                                 jnp.concatenate([v_hi, ones_hi], axis=1)], axis=0)
        res = jnp.dot(p_cat, v_cat, preferred_element_type=_F32)
        return jnp.where(first_half, m[0:BQ], m[BQ:2 * BQ]), res[:, LANES:], res[:, 0:LANES]

    def stacked_q(block):
        return jnp.concatenate([block[:, 0:LANES], block[:, LANES:2 * LANES]], axis=0)

    def merge(pair, rows, m_blk, l_blk, pv, last):
        lse_old = lse_ref[pair, rows, :]
        m_new = jnp.maximum(lse_old, m_blk)
        a = jnp.exp2(lse_old - m_new)
        b = jnp.exp2(m_blk - m_new)
        l_new = a + b * l_blk
        acc_ref[pair, rows, :] = (a * acc_ref[pair, rows, :] + b * pv) / l_new
        if not last:
            lse_ref[pair, rows, :] = m_new + jnp.log2(l_new)

    def pair_unit(cur, prev, nxt, i, pair, bias):
        q0 = KV_W + 2 * pair * LANES
        kw = window(cur, prev, nxt, i, slice(pair * LANES, (pair + 1) * LANES))
        vw = window(cur, prev, nxt, i, slice(k_w + pair * LANES, k_w + (pair + 1) * LANES))
        return unit(stacked_q(cur[i * BQ:(i + 1) * BQ, q0:q0 + 2 * LANES]), kw,
                    vw * masks[_LO], vw * masks[_HI], bias)

    def group0(i):
        row0 = i * BQ
        bias = bias_for(t * DIL_TILE + row0 - B_HALO, seq_len, i, rows0 // BQ)
        for pair in range(n_pairs):
            m_blk, l_blk, pv = pair_unit(c0, p0, n0, i, pair, bias)
            rows = pl.ds(row0, BQ)
            lse_ref[pair, rows, :] = m_blk + jnp.log2(l_blk)
            acc_ref[pair, rows, :] = pv / l_blk

    def group1(r, i):
        row0 = i * BQ
        bias = bias_for(t * rows1 + row0 - B_HALO, seq_len // d1, i, rows1 // BQ)
        for pair in range(n_pairs):
            m_blk, l_blk, pv = pair_unit(c1.at[r], p1.at[r], n1.at[r], i, pair, bias)
            merge(pair, pl.ds(row0 * d1 + r, BQ, stride=d1), m_blk, l_blk, pv, last=False)

    bias2 = bias_for(t * BQ - B_HALO, seq_len // d2, 0, 1)

    def group2(r):
        for pair in range(n_pairs):
            m_blk, l_blk, pv = pair_unit(c2.at[r], p2.at[r], n2.at[r], 0, pair, bias2)
            merge(pair, pl.ds(r, BQ, stride=d2), m_blk, l_blk, pv, last=True)

    for i in range(rows0 // BQ):
        group0(i)
    for r in range(d1):
        for i in range(rows1 // BQ):
            group1(r, i)
    for r in range(d2):
        group2(r)

    def finish(i, carry):
        rows = pl.ds(pl.multiple_of(i * BQ, BQ), BQ)
        for pair in range(n_pairs):
            o_ref[rows, pair * LANES:(pair + 1) * LANES] = acc_ref[pair, rows, :].astype(_BF)
        return carry

    lax.fori_loop(0, DIL_TILE // BQ, finish, 0)


def _dilated_call(g0, g1, g2, batch, seq):
    n_t = seq // DIL_TILE
    d1, d2 = B_GROUPS[1][1], B_GROUPS[2][1]
    rows1, rows2 = DIL_TILE // d1, DIL_TILE // d2
    hb0 = DIL_TILE // B_HALO
    last0 = seq // B_HALO - 1
    prev_t = lambda t: jnp.maximum(t - 1, 0)
    next_t = lambda t: jnp.minimum(t + 1, n_t - 1)
    in_specs = [
        pl.BlockSpec((None, DIL_TILE, SLAB), lambda b, t: (b, t, 0)),
        pl.BlockSpec((None, B_HALO, KV_W), lambda b, t: (b, jnp.maximum(t * hb0 - 1, 0), 0)),
        pl.BlockSpec((None, B_HALO, KV_W), lambda b, t: (b, jnp.minimum((t + 1) * hb0, last0), 0)),
        pl.BlockSpec((None, None, d1, rows1, SLAB), lambda b, t: (b, t, 0, 0, 0)),
        pl.BlockSpec((None, None, d1, B_HALO, KV_W),
                     lambda b, t: (b, prev_t(t), 0, rows1 // B_HALO - 1, 0)),
        pl.BlockSpec((None, None, d1, B_HALO, KV_W), lambda b, t: (b, next_t(t), 0, 0, 0)),
        pl.BlockSpec((None, None, d2, rows2, SLAB), lambda b, t: (b, t, 0, 0, 0)),
        pl.BlockSpec((None, None, d2, B_HALO, KV_W),
                     lambda b, t: (b, prev_t(t), 0, rows2 // B_HALO - 1, 0)),
        pl.BlockSpec((None, None, d2, B_HALO, KV_W), lambda b, t: (b, next_t(t), 0, 0, 0)),
        pl.BlockSpec((4, B_BK, BQ), lambda b, t: (0, 0, 0)),
        pl.BlockSpec((2 * BQ, BQ), lambda b, t: (0, 0)),
    ]
    n_pairs = B_OUT_W // LANES
    state = pltpu.VMEM((n_pairs, DIL_TILE, LANES), _F32)
    return pl.pallas_call(
        functools.partial(_dilated_kernel, seq_len=seq),
        grid=(batch, n_t),
        in_specs=in_specs,
        out_specs=pl.BlockSpec((None, DIL_TILE, B_OUT_W), lambda b, t: (b, t, 0)),
        out_shape=jax.ShapeDtypeStruct((batch, seq, B_OUT_W), _BF),
        scratch_shapes=[state, state],
        compiler_params=pltpu.CompilerParams(
            dimension_semantics=("arbitrary",) * 2, vmem_limit_bytes=VMEM_LIMIT),
        name="attn_dilated",
    )(g0, g0, g0, g1, g1, g1, g2, g2, g2, _bias_variants(B_HALO, 1), _one_hot_rows(2 * BQ))


def _out_kernel(x_ref, oa_ref, ob_ref, n1_ref, wga_ref, wgb_ref, wa_ref, wb_ref, wo_ref,
                n2_ref, wg_ref, wu_ref, wd_ref, y_ref, mix_ref, act_ref):
    x = x_ref[...]
    h1 = _rms_rows(x, n1_ref[...]).astype(_BF)
    oa = oa_ref[...]
    ob = ob_ref[...]
    for c in range(D_MODEL // MXU_N):
        cs = slice(c * MXU_N, (c + 1) * MXU_N)
        ga = jax.nn.sigmoid(jnp.dot(h1, wga_ref[:, cs], preferred_element_type=_F32))
        gb = jax.nn.sigmoid(jnp.dot(h1, wgb_ref[:, cs], preferred_element_type=_F32))
        ya = jnp.dot(oa, wa_ref[:, cs], preferred_element_type=_F32)
        yb = jnp.dot(ob, wb_ref[:, cs], preferred_element_type=_F32)
        mix_ref[:, cs] = (ga * ya + gb * yb).astype(_BF)
    x1 = x + jnp.dot(mix_ref[...], wo_ref[...], preferred_element_type=_F32)
    h2 = _rms_rows(x1, n2_ref[...]).astype(_BF)
    for c in range(D_FF // MXU_N):
        cs = slice(c * MXU_N, (c + 1) * MXU_N)
        gate = jnp.dot(h2, wg_ref[:, cs], preferred_element_type=_F32)
        up = jnp.dot(h2, wu_ref[:, cs], preferred_element_type=_F32)
        act_ref[:, cs] = (jax.nn.silu(gate) * up).astype(_BF)
    y_ref[...] = x1 + jnp.dot(act_ref[...], wd_ref[...], preferred_element_type=_F32)


def _out_call(x2d, oa, ob, n1, wga, wgb, wa, wb, wo, n2, wg, wu, wd):
    tokens = x2d.shape[0]
    tm = TOKEN_TILE
    row = lambda i: (i, 0)
    const = lambda i: (0, 0)
    once = dict(pipeline_mode=pl.Buffered(1))
    return pl.pallas_call(
        _out_kernel,
        grid=(tokens // tm,),
        in_specs=[
            pl.BlockSpec((tm, D_MODEL), row),
            pl.BlockSpec((tm, A_Q_W), row),
            pl.BlockSpec((tm, B_OUT_W), row),
            pl.BlockSpec((1, D_MODEL), const),
            pl.BlockSpec((D_MODEL, D_MODEL), const, **once),
            pl.BlockSpec((D_MODEL, D_MODEL), const, **once),
            pl.BlockSpec((A_Q_W, D_MODEL), const, **once),
            pl.BlockSpec((B_OUT_W, D_MODEL), const, **once),
            pl.BlockSpec((D_MODEL, D_MODEL), const, **once),
            pl.BlockSpec((1, D_MODEL), const),
            pl.BlockSpec((D_MODEL, D_FF), const, **once),
            pl.BlockSpec((D_MODEL, D_FF), const, **once),
            pl.BlockSpec((D_FF, D_MODEL), const, **once),
        ],
        out_specs=pl.BlockSpec((tm, D_MODEL), row),
        out_shape=jax.ShapeDtypeStruct((tokens, D_MODEL), _F32),
        scratch_shapes=[pltpu.VMEM((tm, D_MODEL), _BF), pltpu.VMEM((tm, D_FF), _BF)],
        compiler_params=pltpu.CompilerParams(
            dimension_semantics=("arbitrary",), vmem_limit_bytes=VMEM_LIMIT),
        name="out_ffn",
    )(x2d, oa, ob, n1, wga, wgb, wa, wb, wo, n2, wg, wu, wd)


def _prepare_weights(norm1, w_in, qn_a, kn_a, sink_a, qn_b, kn_b, w_br_a, w_br_b, w_out,
                     norm2, w_gate, w_up, w_down):
    w_in0 = w_in[0]
    gate0 = QKV_W
    dims = _pair_lane_dims()
    gains = jnp.stack([qn_a[0][dims] * Q_SCALE, kn_a[0][dims],
                       qn_b[0][dims] * Q_SCALE, kn_b[0][dims]]).astype(_F32)
    sink = sink_a[0].astype(_F32) * math.log2(math.e)
    half_heads = A_Q_HEADS // 2
    sink_pairs = jnp.stack([jnp.repeat(sink[:half_heads], HEAD_DIM).reshape(half_heads, HEAD_DIM),
                            jnp.repeat(sink[half_heads:], HEAD_DIM).reshape(half_heads, HEAD_DIM)],
                           axis=1).reshape(1, A_Q_W)
    return dict(
        n1=norm1[0].reshape(1, D_MODEL).astype(_F32),
        w_qkv=w_in0[:, _qkv_columns()].astype(_BF),
        gains=gains,
        sink=sink_pairs,
        wga=w_in0[:, gate0:gate0 + D_MODEL].astype(_BF),
        wgb=w_in0[:, gate0 + D_MODEL:gate0 + 2 * D_MODEL].astype(_BF),
        wa=w_br_a[0][_oa_rows(), :].astype(_BF),
        wb=w_br_b[0].astype(_BF),
        wo=w_out[0].astype(_BF),
        n2=norm2[0].reshape(1, D_MODEL).astype(_F32),
        wg=w_gate[0].astype(_BF),
        wu=w_up[0].astype(_BF),
        wd=w_down[0].astype(_BF),
    )


def _trunk(x, wts, mean_mat, cos, sin):
    batch, seq, _ = x.shape
    assert seq % DIL_TILE == 0 and seq >= 2 * BQ and seq <= cos.shape[0]
    tokens = batch * seq
    n_t = seq // DIL_TILE
    x2d = x.reshape(tokens, D_MODEL)
    qkv_a, g0, g1, g2 = _qkv_call(x2d, seq, wts["n1"], wts["w_qkv"], wts["gains"], cos, sin,
                                  mean_mat)
    oa = _window_call(qkv_a.reshape(batch, seq, SLAB), wts["sink"])
    ob = _dilated_call(g0.reshape(batch, seq, SLAB),
                       g1.reshape((batch, n_t) + g1.shape[1:]),
                       g2.reshape((batch, n_t) + g2.shape[1:]), batch, seq)
    y = _out_call(x2d, oa.reshape(tokens, A_Q_W), ob.reshape(tokens, B_OUT_W),
                  wts["n1"], wts["wga"], wts["wgb"], wts["wa"], wts["wb"], wts["wo"],
                  wts["n2"], wts["wg"], wts["wu"], wts["wd"])
    return y.reshape(batch, seq, D_MODEL)


def kernel(x_prompt, x_sample, norm1, w_in, qn_a, kn_a, sink_a, qn_b, kn_b, w_br_a, w_br_b,
           w_out, norm2, w_gate, w_up, w_down):
    wts = _prepare_weights(norm1, w_in, qn_a, kn_a, sink_a, qn_b, kn_b, w_br_a, w_br_b, w_out,
                           norm2, w_gate, w_up, w_down)
    mean_mat = _head_mean_matrix()
    cos, sin = _rope_tables(max(x_prompt.shape[1], x_sample.shape[1]))
    return (_trunk(x_prompt, wts, mean_mat, cos, sin), _trunk(x_sample, wts, mean_mat, cos, sin))
```
